```python
import jax
import jax.numpy as jnp
from jax import lax
import numpy as np

D_MODEL = 2048
BATCH = 1
SEQ = 16384
DEPTH = 1
DEC_BATCH = 32
DEC_SEQ = 64
PAST_LEN = 4096

CHUNK = 64
N_META = 16
Q_BLOCK = 128
PROMPT_PAD = Q_BLOCK - N_META
SB_HEAD_DIM = 128
SB_WIDTH = D_MODEL // 2
SB_HEADS = SB_WIDTH // SB_HEAD_DIM
GLA_WIDTH = D_MODEL - SB_WIDTH
GLA_HEADS = 4
GLA_DV = GLA_WIDTH // GLA_HEADS
GLA_KEY_WIDTH = GLA_WIDTH // 2
GLA_DK = GLA_KEY_WIDTH // GLA_HEADS
GLA_GATE_RANK = 16
GLA_GATE_TAU = 16.0
N_EXPERTS = 32
TOP_K = 4
D_FF = D_MODEL
SWIGLU_LIMIT = 7.0
SWIGLU_ALPHA = 1.702
MOE_BLOCK = 128
RMS_EPS = 1e-5
IN_SPLITS = (SB_WIDTH, SB_WIDTH, SB_WIDTH, GLA_KEY_WIDTH, GLA_KEY_WIDTH, GLA_WIDTH, GLA_WIDTH, GLA_GATE_RANK)
D_IN = SB_WIDTH * 3 + GLA_KEY_WIDTH * 2 + GLA_WIDTH * 2 + GLA_GATE_RANK

kernel_name = 'hybrid_stickbreak_gla_moe_stream_step'


def rmsnorm(x, gain):
    xf = x.astype(jnp.float32)
    inv = lax.rsqrt(jnp.mean(xf * xf, axis=-1, keepdims=True) + RMS_EPS)
    return (xf * inv).astype(x.dtype) * gain


def project(hn, w_in, w_gk2, b_gk):
    bsz, length, _ = hn.shape
    offsets = np.cumsum(IN_SPLITS)[:-1].tolist()
    q_sb, k_sb, v_sb, q_g, k_g, v_g, g_g, a_lr = jnp.split(hn @ w_in, offsets, axis=-1)

    def heads(t, n):
        return t.reshape(bsz, length, n, -1)

    lg = jax.nn.log_sigmoid((a_lr @ w_gk2 + b_gk).astype(jnp.float32)) / GLA_GATE_TAU
    return (heads(q_sb, SB_HEADS), heads(k_sb, SB_HEADS), heads(v_sb, SB_HEADS),
            heads(q_g, GLA_HEADS) * (GLA_DK ** -0.5), heads(k_g, GLA_HEADS), heads(v_g, GLA_HEADS),
            heads(g_g, GLA_HEADS), heads(lg, GLA_HEADS))


def stick_breaking(q, k, v, q_pos, k_pos, k_valid):
    z = jnp.einsum('bqhd,bkhd->bhqk', q, k).astype(jnp.float32) * (SB_HEAD_DIM ** -0.5)
    allowed = (k_pos[None, :] < q_pos[:, None]) & k_valid[None, :]
    log_fail = jnp.where(allowed, jax.nn.log_sigmoid(-z), 0.0)
    later = lax.cumsum(log_fail, axis=3, reverse=True) - log_fail
    w = jnp.where(allowed, jnp.exp(jax.nn.log_sigmoid(z) + later), 0.0)
    return jnp.einsum('bhqk,bkhd->bqhd', w.astype(v.dtype), v)


def sb_prompt(q, k, v, valid):
    bsz, length, nh, dh = q.shape
    nb = length // Q_BLOCK
    pos = jnp.arange(length)
    q_blocks = q.reshape(bsz, nb, Q_BLOCK, nh, dh).swapaxes(0, 1)
    p_blocks = pos.reshape(nb, Q_BLOCK)
    out = lax.map(lambda a: stick_breaking(a[0], k, v, a[1], pos, valid), (q_blocks, p_blocks))
    return out.swapaxes(0, 1).reshape(bsz, length, nh, dh)


def gla_chunk(state, q, k, v, lg):
    c = q.shape[1]
    qf, kf, vf = q.astype(jnp.float32), k.astype(jnp.float32), v.astype(jnp.float32)
    b = jnp.cumsum(lg, axis=1)
    causal = jnp.tril(jnp.ones((c, c), bool))
    diff = b[:, :, None] - b[:, None, :]
    decay = jnp.exp(jnp.where(causal[None, :, :, None, None], diff, -jnp.inf))
    att = jnp.einsum('bthk,bshk,btshk->bhts', qf, kf, decay)
    out = (jnp.einsum('bthk,bhkv->bthv', qf * jnp.exp(b), state)
           + jnp.einsum('bhts,bshv->bthv', att, vf))
    b_last = b[:, -1]
    new_state = (jnp.exp(b_last)[..., None] * state
                 + jnp.einsum('bshk,bshv->bhkv', kf * jnp.exp(b_last[:, None] - b), vf))
    return new_state, out


def gla_prompt(q, k, v, lg):
    bsz, length = q.shape[:2]
    n = length // CHUNK

    def split(t):
        return t.reshape(bsz, n, CHUNK, *t.shape[2:]).swapaxes(0, 1)

    s0 = jnp.zeros((bsz, GLA_HEADS, GLA_DK, GLA_DV), jnp.float32)
    s_final, out = lax.scan(lambda s, a: gla_chunk(s, *a), s0, (split(q), split(k), split(v), split(lg)))
    return s_final, out.swapaxes(0, 1).reshape(bsz, length, GLA_HEADS, GLA_DV)


def mix_out(o_sb, o_g, g_g, gla_norm, w_out):
    bsz, length = o_sb.shape[:2]
    o_g = rmsnorm(o_g, gla_norm) * jax.nn.silu(g_g.astype(jnp.float32))
    cat = jnp.concatenate([o_sb.reshape(bsz, length, SB_WIDTH),
                           o_g.reshape(bsz, length, GLA_WIDTH).astype(o_sb.dtype)], axis=-1)
    return cat @ w_out


def moe_ffn(x, w_router, b_router, w_gate_up, b_gate_up, w_down, b_down):
    shp = x.shape
    xt = x.reshape(-1, shp[-1])
    t = xt.shape[0]
    logits = xt.astype(jnp.float32) @ w_router.astype(jnp.float32) + b_router.astype(jnp.float32)
    top_val, top_idx = lax.top_k(logits, TOP_K)
    gate = jax.nn.softmax(top_val, axis=-1)
    n_assign = t * TOP_K
    expert = top_idx.reshape(n_assign)
    token = jnp.arange(n_assign, dtype=jnp.int32) // TOP_K
    order = jnp.argsort(expert)
    e_sorted = expert[order]
    counts = jnp.bincount(expert, length=N_EXPERTS)
    padded = (counts + MOE_BLOCK - 1) // MOE_BLOCK * MOE_BLOCK
    start = jnp.cumsum(counts) - counts
    ends_p = jnp.cumsum(padded)
    pstart = ends_p - padded
    dest = pstart[e_sorted] + jnp.arange(n_assign) - start[e_sorted]
    n_blocks = -(-n_assign // MOE_BLOCK) + N_EXPERTS
    n_slots = n_blocks * MOE_BLOCK
    slot_token = jnp.full((n_slots,), t, jnp.int32).at[dest].set(token[order])
    slot_gate = jnp.zeros((n_slots,), jnp.float32).at[dest].set(gate.reshape(n_assign)[order])
    block_expert = jnp.minimum(
        jnp.searchsorted(ends_p, jnp.arange(n_blocks) * MOE_BLOCK, side='right'), N_EXPERTS - 1)
    x_pad = jnp.concatenate([xt, jnp.zeros((1, shp[-1]), xt.dtype)], axis=0)
    xb = x_pad[slot_token].reshape(n_blocks, MOE_BLOCK, shp[-1])

    def run(a):
        xblk, e = a
        gu = xblk @ w_gate_up[e] + b_gate_up[e]
        g, u = jnp.split(gu, 2, axis=-1)
        g = jnp.minimum(g, SWIGLU_LIMIT)
        u = jnp.clip(u, -SWIGLU_LIMIT, SWIGLU_LIMIT)
        return ((u + 1) * (g * jax.nn.sigmoid(g * SWIGLU_ALPHA))) @ w_down[e] + b_down[e]

    yb = lax.map(run, (xb, block_expert))
    y = jnp.zeros((t + 1, shp[-1]), jnp.float32).at[slot_token].add(
        yb.reshape(n_slots, shp[-1]).astype(jnp.float32) * slot_gate[:, None])
    return y[:t].astype(x.dtype).reshape(shp)


def setup_inputs(seed: int = 0) -> dict:
    key = jax.random.key(seed)
    ks = jax.random.split(key, 20)
    f32 = jnp.float32

    def nrm(k, shape, scale=1.0):
        return jax.random.normal(k, shape, f32) * scale

    return {
        'x_prompt': nrm(ks[0], (BATCH, SEQ, D_MODEL)),
        'x_sample': nrm(ks[1], (DEC_BATCH, DEC_SEQ, D_MODEL)),
        'cache_sb_k': nrm(ks[2], (DEPTH, DEC_BATCH, PAST_LEN, SB_HEADS, SB_HEAD_DIM)),
        'cache_sb_v': nrm(ks[3], (DEPTH, DEC_BATCH, PAST_LEN, SB_HEADS, SB_HEAD_DIM)),
        'state_gla': nrm(ks[4], (DEPTH, DEC_BATCH, GLA_HEADS, GLA_DK, GLA_DV), 0.5),
        'meta_tokens': nrm(ks[5], (N_META, D_MODEL)),
        'norm_mix': 1.0 + nrm(ks[6], (DEPTH, D_MODEL), 0.01),
        'w_in': nrm(ks[7], (DEPTH, D_MODEL, D_IN), D_MODEL ** -0.5),
        'w_gk2': nrm(ks[8], (DEPTH, GLA_GATE_RANK, GLA_KEY_WIDTH), GLA_GATE_RANK ** -0.5),
        'b_gk': nrm(ks[9], (DEPTH, GLA_KEY_WIDTH), 0.1),
        'gla_norm': 1.0 + nrm(ks[10], (DEPTH, GLA_DV), 0.01),
        'w_out': nrm(ks[11], (DEPTH, SB_WIDTH + GLA_WIDTH, D_MODEL), D_MODEL ** -0.5),
        'norm_ffn': 1.0 + nrm(ks[12], (DEPTH, D_MODEL), 0.01),
        'w_router': nrm(ks[13], (DEPTH, D_MODEL, N_EXPERTS), D_MODEL ** -0.5),
        'b_router': nrm(ks[14], (DEPTH, N_EXPERTS), 0.01),
        'w_gate_up': nrm(ks[15], (DEPTH, N_EXPERTS, D_MODEL, 2 * D_FF), D_MODEL ** -0.5),
        'b_gate_up': nrm(ks[16], (DEPTH, N_EXPERTS, 2 * D_FF), 0.01),
        'w_down': nrm(ks[17], (DEPTH, N_EXPERTS, D_FF, D_MODEL), D_FF ** -0.5),
        'b_down': nrm(ks[18], (DEPTH, N_EXPERTS, D_MODEL), 0.01),
        'norm_final': 1.0 + nrm(ks[19], (D_MODEL,), 0.01),
    }


def reference(x_prompt, x_sample, cache_sb_k, cache_sb_v, state_gla, meta_tokens, norm_mix, w_in,
              w_gk2, b_gk, gla_norm, w_out, norm_ffn, w_router, b_router, w_gate_up, b_gate_up,
              w_down, b_down, norm_final):
    bsz = x_prompt.shape[0]
    pad = jnp.zeros((bsz, PROMPT_PAD, D_MODEL), x_prompt.dtype)
    meta = jnp.broadcast_to(meta_tokens.astype(x_prompt.dtype)[None], (bsz, N_META, D_MODEL))
    h = jnp.concatenate([pad, meta, x_prompt], axis=1)
    valid = jnp.arange(h.shape[1]) >= PROMPT_PAD
    vmask = valid[None, :, None, None]

    hs = x_sample
    n_new = hs.shape[1]
    past = cache_sb_k.shape[2]
    q_pos_s = past + jnp.arange(n_new)
    k_pos_s = jnp.arange(past + n_new)
    k_valid_s = jnp.ones((past + n_new,), bool)

    kp, vp, sp, ks, vs, ss = [], [], [], [], [], []
    for layer in range(DEPTH):
        q_sb, k_sb, v_sb, q_g, k_g, v_g, g_g, lg = project(
            rmsnorm(h, norm_mix[layer]), w_in[layer], w_gk2[layer], b_gk[layer])
        k_g = jnp.where(vmask, k_g, 0)
        lg = jnp.where(vmask, lg, 0.0)
        o_sb = sb_prompt(q_sb, k_sb, v_sb, valid)
        s_p, o_g = gla_prompt(q_g, k_g, v_g, lg)
        h = h + mix_out(o_sb, o_g, g_g, gla_norm[layer], w_out[layer])
        h = h + moe_ffn(rmsnorm(h, norm_ffn[layer]), w_router[layer], b_router[layer],
                        w_gate_up[layer], b_gate_up[layer], w_down[layer], b_down[layer])
        kp.append(k_sb[:, PROMPT_PAD:])
        vp.append(v_sb[:, PROMPT_PAD:])
        sp.append(s_p)

        q_sb, k_sb, v_sb, q_g, k_g, v_g, g_g, lg = project(
            rmsnorm(hs, norm_mix[layer]), w_in[layer], w_gk2[layer], b_gk[layer])
        k_all = jnp.concatenate([cache_sb_k[layer].astype(k_sb.dtype), k_sb], axis=1)
        v_all = jnp.concatenate([cache_sb_v[layer].astype(v_sb.dtype), v_sb], axis=1)
        o_sb = stick_breaking(q_sb, k_all, v_all, q_pos_s, k_pos_s, k_valid_s)
        s_s, o_g = gla_chunk(state_gla[layer].astype(jnp.float32), q_g, k_g, v_g, lg)
        hs = hs + mix_out(o_sb, o_g, g_g, gla_norm[layer], w_out[layer])
        hs = hs + moe_ffn(rmsnorm(hs, norm_ffn[layer]), w_router[layer], b_router[layer],
                          w_gate_up[layer], b_gate_up[layer], w_down[layer], b_down[layer])
        ks.append(k_sb)
        vs.append(v_sb)
        ss.append(s_s)

    y_prompt = rmsnorm(h, norm_final)[:, PROMPT_PAD + N_META:]
    y_sample = rmsnorm(hs, norm_final)
    return (y_prompt, y_sample, jnp.stack(kp), jnp.stack(vp), jnp.stack(sp),
            jnp.stack(ks), jnp.stack(vs), jnp.stack(ss))
```

```python
import functools

import jax
import jax.numpy as jnp
import numpy as np
from jax import lax
from jax.experimental import pallas as pl
from jax.experimental.pallas import tpu as pltpu

F32 = jnp.float32
BF16 = jnp.bfloat16
HIGHEST = lax.Precision.HIGHEST

D_MODEL = 2048
N_META = 16
META_BLOCK = 128
SB_HEADS = 8
SB_DIM = 128
SB_WIDTH = SB_HEADS * SB_DIM
GLA_HEADS = 4
GLA_DK = 128
GLA_DV = 256
GLA_KEY_WIDTH = GLA_HEADS * GLA_DK
GLA_WIDTH = GLA_HEADS * GLA_DV
GLA_RANK = 16
GLA_TAU = 16.0
GLA_CHUNK = 64
N_EXPERTS = 32
TOP_K = 4
D_FF = 2048
SWIGLU_LIMIT = 7.0
SWIGLU_ALPHA = 1.702
RMS_EPS = 1e-5
D_MAIN = 3 * SB_WIDTH + 2 * GLA_KEY_WIDTH + 2 * GLA_WIDTH
LANES = 128

SB_TQ = 512
SB_TK = 256
MOE_TM = 256
VMEM_LIMIT = 56 * 1024 * 1024


def _cparams(sem):
    return pltpu.CompilerParams(dimension_semantics=sem, vmem_limit_bytes=VMEM_LIMIT)


def _pick_tile(n, cands):
    for c in cands:
        if n % c == 0:
            return c
    raise ValueError(f"no tile for {n} in {cands}")


def _softplus(z):
    return jnp.maximum(z, 0.0) + jnp.log(1.0 + jnp.exp(-jnp.abs(z)))


def _inproj_kernel(x_ref, gain_ref, w_ref, scale_ref, wa_ref, wgk_ref, bgk_ref,
                   p32_ref, p16_ref, lg_ref, xn_ref):
    @pl.when(pl.program_id(1) == 0)
    def _():
        x = x_ref[...]
        inv = lax.rsqrt(jnp.mean(x * x, axis=-1, keepdims=True) + RMS_EPS)
        xn = ((x * inv) * gain_ref[...]).astype(BF16)
        xn_ref[...] = xn
        a = jnp.dot(xn, wa_ref[...], preferred_element_type=F32)
        pre = jnp.dot(a, wgk_ref[...], preferred_element_type=F32, precision=HIGHEST) + bgk_ref[...]
        lg_ref[...] = -_softplus(-pre) * (1.0 / GLA_TAU)

    y = jnp.dot(xn_ref[...], w_ref[...], preferred_element_type=F32) * scale_ref[...]
    p32_ref[...] = y
    p16_ref[...] = y.astype(BF16)


def _inproj(x_all, gain, w_main, colscale, w_a, w_gk, b_gk):
    rows = x_all.shape[0]
    tm = _pick_tile(rows, (640, 512, 256, 128))
    tn = 1024
    return pl.pallas_call(
        _inproj_kernel,
        grid=(rows // tm, D_MAIN // tn),
        in_specs=[
            pl.BlockSpec((tm, D_MODEL), lambda i, j: (i, 0)),
            pl.BlockSpec((1, D_MODEL), lambda i, j: (0, 0)),
            pl.BlockSpec((D_MODEL, tn), lambda i, j: (0, j)),
            pl.BlockSpec((1, tn), lambda i, j: (0, j)),
            pl.BlockSpec((D_MODEL, LANES), lambda i, j: (0, 0)),
            pl.BlockSpec((LANES, GLA_KEY_WIDTH), lambda i, j: (0, 0)),
            pl.BlockSpec((1, GLA_KEY_WIDTH), lambda i, j: (0, 0)),
        ],
        out_specs=[
            pl.BlockSpec((tm, tn), lambda i, j: (i, j)),
            pl.BlockSpec((tm, tn), lambda i, j: (i, j)),
            pl.BlockSpec((tm, GLA_KEY_WIDTH), lambda i, j: (i, 0)),
        ],
        out_shape=[
            jax.ShapeDtypeStruct((rows, D_MAIN), F32),
            jax.ShapeDtypeStruct((rows, D_MAIN), BF16),
            jax.ShapeDtypeStruct((rows, GLA_KEY_WIDTH), F32),
        ],
        scratch_shapes=[pltpu.VMEM((tm, D_MODEL), BF16)],
        compiler_params=_cparams(("arbitrary", "arbitrary")),
    )(x_all, gain, w_main, colscale, w_a, w_gk, b_gk)


def _sb_step(q, kblk, vblk, u, c, allowed):
    z = lax.dot_general(q, kblk, (((1,), (1,)), ((), ())), preferred_element_type=F32)
    sp = _softplus(z)
    if allowed is not None:
        sp = jnp.where(allowed, sp, 0.0)
    hi = sp.astype(BF16)
    lo = (sp - hi.astype(F32)).astype(BF16)
    r = (jnp.dot(hi, u, preferred_element_type=F32)
         + jnp.dot(lo, u, preferred_element_type=F32))
    w = jnp.exp(z - r - c)
    if allowed is not None:
        w = jnp.where(allowed, w, 0.0)
    pv = jnp.dot(w.astype(BF16), vblk, preferred_element_type=F32)
    return pv, c + r[:, 0:1]


def _sb_prompt_kernel(q_ref, k_ref, v_ref, km_ref, vm_ref, u_ref, o_ref, acc_ref, c_ref):
    i = pl.program_id(1)
    q = q_ref[pl.ds(pl.multiple_of(i * SB_TQ, SB_TQ), SB_TQ), :]
    u = u_ref[...]
    acc_ref[...] = jnp.zeros_like(acc_ref)
    c_ref[...] = jnp.zeros_like(c_ref)

    def visit(kblk, vblk, ublk, allowed):
        pv, c_new = _sb_step(q, kblk, vblk, ublk, c_ref[...], allowed)
        acc_ref[...] += pv
        c_ref[...] = c_new

    rows = lax.broadcasted_iota(jnp.int32, (SB_TQ, SB_TK), 0)
    cols = lax.broadcasted_iota(jnp.int32, (SB_TQ, SB_TK), 1)
    for sub in reversed(range(SB_TQ // SB_TK)):
        start = pl.multiple_of(i * SB_TQ + sub * SB_TK, SB_TK)
        visit(k_ref[pl.ds(start, SB_TK), :], v_ref[pl.ds(start, SB_TK), :], u,
              cols + sub * SB_TK < rows)

    n_full = i * (SB_TQ // SB_TK)

    def body(n, carry):
        start = pl.multiple_of((n_full - 1 - n) * SB_TK, SB_TK)
        visit(k_ref[pl.ds(start, SB_TK), :], v_ref[pl.ds(start, SB_TK), :], u, None)
        return carry

    lax.fori_loop(0, n_full, body, 0)

    meta_cols = lax.broadcasted_iota(jnp.int32, (SB_TQ, META_BLOCK), 1)
    visit(km_ref[...], vm_ref[...], u[:META_BLOCK, :META_BLOCK], meta_cols < N_META)
    o_ref[...] = acc_ref[...].astype(o_ref.dtype)


def _sb_prompt(p16, u, n_prompt, n_tok):
    head_spec = lambda off: pl.BlockSpec((n_prompt, SB_DIM), lambda h, i: (0, off + h))
    meta_spec = lambda off: pl.BlockSpec((META_BLOCK, SB_DIM), lambda h, i: (n_tok // META_BLOCK, off + h))
    return pl.pallas_call(
        _sb_prompt_kernel,
        grid=(SB_HEADS, n_prompt // SB_TQ),
        in_specs=[head_spec(0), head_spec(SB_HEADS), head_spec(2 * SB_HEADS),
                  meta_spec(SB_HEADS), meta_spec(2 * SB_HEADS),
                  pl.BlockSpec((SB_TK, SB_TK), lambda h, i: (0, 0))],
        out_specs=pl.BlockSpec((SB_TQ, SB_DIM), lambda h, i: (i, h)),
        out_shape=jax.ShapeDtypeStruct((n_prompt, SB_WIDTH), BF16),
        scratch_shapes=[pltpu.VMEM((SB_TQ, SB_DIM), F32), pltpu.VMEM((SB_TQ, 1), F32)],
        compiler_params=_cparams(("arbitrary", "arbitrary")),
    )(p16, p16, p16, p16, p16, u)


def _sb_sample_kernel(q_ref, kn_ref, vn_ref, kc_ref, vc_ref, u_ref, o_ref, *, n_new, n_past):
    q = q_ref[...]
    u = u_ref[...]
    rows = lax.broadcasted_iota(jnp.int32, (n_new, n_new), 0)
    cols = lax.broadcasted_iota(jnp.int32, (n_new, n_new), 1)
    acc, c = _sb_step(q, kn_ref[...], vn_ref[...], u[:n_new, :n_new],
                      jnp.zeros((n_new, 1), F32), cols < rows)
    for blk in reversed(range(n_past // SB_TK)):
        kblk = kc_ref[blk * SB_TK:(blk + 1) * SB_TK, :].astype(BF16)
        vblk = vc_ref[blk * SB_TK:(blk + 1) * SB_TK, :].astype(BF16)
        pv, c = _sb_step(q, kblk, vblk, u, c, None)
        acc = acc + pv
    o_ref[...] = acc.astype(o_ref.dtype)


def _sb_sample(p16, cache_k, cache_v, u, n_prompt, n_batch, n_new):
    n_past = cache_k.shape[1]
    row0 = n_prompt // n_new
    new_spec = lambda off: pl.BlockSpec((n_new, SB_DIM), lambda b, h: (row0 + b, off + h))
    cache_spec = pl.BlockSpec((None, n_past, SB_DIM), lambda b, h: (b, 0, h))
    return pl.pallas_call(
        functools.partial(_sb_sample_kernel, n_new=n_new, n_past=n_past),
        grid=(n_batch, SB_HEADS),
        in_specs=[new_spec(0), new_spec(SB_HEADS), new_spec(2 * SB_HEADS), cache_spec, cache_spec,
                  pl.BlockSpec((SB_TK, SB_TK), lambda b, h: (0, 0))],
        out_specs=pl.BlockSpec((n_new, SB_DIM), lambda b, h: (b, h)),
        out_shape=jax.ShapeDtypeStruct((n_batch * n_new, SB_WIDTH), BF16),
        compiler_params=_cparams(("arbitrary", "arbitrary")),
    )(p16, p16, p16, cache_k, cache_v, u)


GLA_LEVELS = 6


def _gla_constants():
    c = GLA_CHUNK
    tri = np.tril(np.ones((c, c), np.float32))
    sel = np.zeros((GLA_LEVELS * c, c), np.float32)
    for lvl in range(GLA_LEVELS):
        n = c >> lvl
        for row in range(c):
            sel[lvl * c + row, (row // n) * n + n // 2 - 1] = 1.0
    return jnp.asarray(tri), jnp.asarray(sel)


def _gla_chunk(q, k, v, lg, st_ref, head, tri, sel):
    c = GLA_CHUNK
    b = jnp.dot(tri, lg, preferred_element_type=F32, precision=HIGHEST)
    refs = jnp.dot(sel, b, preferred_element_type=F32, precision=HIGHEST)
    b_last = b[c - 1:c, :]
    st = st_ref[head]
    o = lax.dot_general((q * jnp.exp(b)).astype(BF16), st.astype(BF16), (((1,), (1,)), ((), ())),
                        preferred_element_type=F32)

    t_row = lax.broadcasted_iota(jnp.int32, (c, 1), 0)
    t_idx = lax.broadcasted_iota(jnp.int32, (c, c), 0)
    s_idx = lax.broadcasted_iota(jnp.int32, (c, c), 1)
    qk = lambda a, bb: lax.dot_general(a.astype(BF16), bb.astype(BF16), (((1,), (1,)), ((), ())),
                                       preferred_element_type=F32)
    att = jnp.where(t_idx == s_idx, qk(q, k), 0.0)
    for lvl in range(GLA_LEVELS):
        half = (c >> lvl) // 2
        shift = GLA_LEVELS - 1 - lvl
        r = refs[lvl * c:(lvl + 1) * c, :]
        late = ((t_row >> shift) & 1) == 1
        qa = jnp.where(late, q * jnp.exp(b - r), 0.0)
        ka = jnp.where(late, 0.0, k * jnp.exp(r - b))
        pair = ((t_idx >> (shift + 1)) == (s_idx >> (shift + 1))) \
            & (((t_idx >> shift) & 1) == 1) & (((s_idx >> shift) & 1) == 0)
        att = att + jnp.where(pair, qk(qa, ka), 0.0)
    o = o + jnp.dot(att.astype(BF16), v.astype(BF16), preferred_element_type=F32)

    kd = (k * jnp.exp(b_last - b)).astype(BF16)
    upd = lax.dot_general(v.astype(BF16), kd, (((0,), (0,)), ((), ())), preferred_element_type=F32)
    st_ref[head] = st * jnp.exp(b_last) + upd
    return o


def _gla_kernel(*refs, prompt):
    if prompt:
        q_ref, k_ref, v_ref, lg_ref, tri_ref, sel_ref, o_ref, s_out_ref, st_ref = refs
    else:
        q_ref, k_ref, v_ref, lg_ref, s_in_ref, tri_ref, sel_ref, o_ref, s_out_ref, st_ref = refs
    step = pl.program_id(0)
    tri = tri_ref[...]
    sel = sel_ref[...]
    if prompt:
        @pl.when(step == 0)
        def _():
            st_ref[...] = jnp.zeros_like(st_ref)
        n_valid = jnp.where(step == 0, N_META, GLA_CHUNK)
        live = lax.broadcasted_iota(jnp.int32, (GLA_CHUNK, 1), 0) < n_valid
    else:
        for head in range(GLA_HEADS):
            st_ref[head] = s_in_ref[head].T
        live = None

    for head in range(GLA_HEADS):
        ks = slice(head * GLA_DK, (head + 1) * GLA_DK)
        vs = slice(head * GLA_DV, (head + 1) * GLA_DV)
        k = k_ref[:, ks]
        lg = lg_ref[:, ks]
        if live is not None:
            k = jnp.where(live, k, 0.0)
            lg = jnp.where(live, lg, 0.0)
        o_ref[:, vs] = _gla_chunk(q_ref[:, ks], k, v_ref[:, vs], lg, st_ref, head, tri, sel)

    def write_state():
        for head in range(GLA_HEADS):
            s_out_ref[head] = st_ref[head].T

    if prompt:
        pl.when(step == pl.num_programs(0) - 1)(write_state)
    else:
        write_state()


def _gla(p32, lg, state_in, tri, sel, *, prompt, n_prompt, n_tok, n_batch):
    c = GLA_CHUNK
    q_blk = 3 * SB_WIDTH // GLA_KEY_WIDTH
    v_blk = (3 * SB_WIDTH + 2 * GLA_KEY_WIDTH) // GLA_WIDTH
    if prompt:
        steps = n_prompt // c + 1
        row = lambda s: jnp.where(s == 0, n_tok // c, s - 1)
        out_row = lambda s: jnp.maximum(s - 1, 0)
        out_rows = n_prompt
        state_idx = lambda s: (0, 0, 0, 0)
        n_state = 1
    else:
        steps = n_batch
        row = lambda s: n_prompt // c + s
        out_row = lambda s: s
        out_rows = n_batch * c
        state_idx = lambda s: (s, 0, 0, 0)
        n_state = n_batch
    state_spec = pl.BlockSpec((None, GLA_HEADS, GLA_DK, GLA_DV), state_idx)
    in_specs = [
        pl.BlockSpec((c, GLA_KEY_WIDTH), lambda s: (row(s), q_blk)),
        pl.BlockSpec((c, GLA_KEY_WIDTH), lambda s: (row(s), q_blk + 1)),
        pl.BlockSpec((c, GLA_WIDTH), lambda s: (row(s), v_blk)),
        pl.BlockSpec((c, GLA_KEY_WIDTH), lambda s: (row(s), 0)),
    ]
    args = [p32, p32, p32, lg]
    if not prompt:
        in_specs.append(state_spec)
        args.append(state_in)
    in_specs += [pl.BlockSpec(tri.shape, lambda s: (0, 0)), pl.BlockSpec(sel.shape, lambda s: (0, 0))]
    args += [tri, sel]
    return pl.pallas_call(
        functools.partial(_gla_kernel, prompt=prompt),
        grid=(steps,),
        in_specs=in_specs,
        out_specs=[pl.BlockSpec((c, GLA_WIDTH), lambda s: (out_row(s), 0)), state_spec],
        out_shape=[jax.ShapeDtypeStruct((out_rows, GLA_WIDTH), F32),
                   jax.ShapeDtypeStruct((n_state, GLA_HEADS, GLA_DK, GLA_DV), F32)],
        scratch_shapes=[pltpu.VMEM((GLA_HEADS, GLA_DV, GLA_DK), F32)],
        compiler_params=_cparams(("arbitrary",)),
    )(*args)


def _mix_kernel(osb_ref, og_ref, gg_ref, x_ref, gnorm_ref, wout_ref, nffn_ref, wr_ref, br_ref, ltri_ref,
                h_ref, xp_ref, idx_ref, gate_ref, rank_ref, cnt_ref, carry_ref):
    step = pl.program_id(0)

    @pl.when(step == 0)
    def _():
        carry_ref[...] = jnp.zeros_like(carry_ref)

    parts = []
    for head in range(GLA_HEADS):
        vs = slice(head * GLA_DV, (head + 1) * GLA_DV)
        og = og_ref[:, vs]
        inv = lax.rsqrt(jnp.mean(og * og, axis=-1, keepdims=True) + RMS_EPS)
        g = gg_ref[:, vs]
        parts.append((((og * inv) * gnorm_ref[...]) * (g * jax.nn.sigmoid(g))).astype(BF16))
    mixed = jnp.dot(osb_ref[...], wout_ref[:SB_WIDTH, :], preferred_element_type=F32)
    for head in range(GLA_HEADS):
        lo = SB_WIDTH + head * GLA_DV
        mixed = mixed + jnp.dot(parts[head], wout_ref[lo:lo + GLA_DV, :], preferred_element_type=F32)
    h = x_ref[...] + mixed
    h_ref[...] = h

    inv = lax.rsqrt(jnp.mean(h * h, axis=-1, keepdims=True) + RMS_EPS)
    xn = (h * inv) * nffn_ref[...]
    half = D_MODEL // 2
    lo_bits = pltpu.bitcast(xn[:, :half].astype(BF16).astype(F32), jnp.uint32)
    hi_bits = pltpu.bitcast(xn[:, half:].astype(BF16).astype(F32), jnp.uint32)
    xp_ref[...] = (hi_bits & jnp.uint32(0xFFFF0000)) | (lo_bits >> 16)

    logits = jnp.dot(xn, wr_ref[...], preferred_element_type=F32, precision=HIGHEST) + br_ref[...]
    tm = logits.shape[0]
    lane = lax.broadcasted_iota(jnp.int32, (tm, LANES), 1)
    vals, sels = [], []
    idx_out = jnp.zeros((tm, LANES), jnp.int32)
    for k in range(TOP_K):
        m = jnp.max(logits, axis=-1, keepdims=True)
        idx = jnp.min(jnp.where(logits == m, lane, LANES), axis=-1, keepdims=True)
        sel = lane == idx
        vals.append(m)
        sels.append(sel)
        idx_out = jnp.where(lane == k, idx, idx_out)
        logits = jnp.where(sel, -jnp.inf, logits)
    exps = [jnp.exp(v - vals[0]) for v in vals]
    denom = exps[0] + exps[1] + exps[2] + exps[3]
    gate_out = jnp.zeros((tm, LANES), F32)
    for k in range(TOP_K):
        gate_out = jnp.where(lane == k, exps[k] / denom, gate_out)
    idx_ref[...] = idx_out
    gate_ref[...] = gate_out

    hot = jnp.zeros((tm, LANES), F32)
    for sel in sels:
        hot = jnp.where(sel, 1.0, hot)
    before = jnp.dot(ltri_ref[...], hot.astype(BF16), preferred_element_type=F32) + carry_ref[...]
    rank_out = jnp.zeros((tm, LANES), jnp.int32)
    for k in range(TOP_K):
        rk = jnp.sum(jnp.where(sels[k], before, 0.0), axis=-1, keepdims=True)
        rank_out = jnp.where(lane == k, rk.astype(jnp.int32), rank_out)
    rank_ref[...] = rank_out
    carry_ref[...] = carry_ref[...] + jnp.sum(hot, axis=0, keepdims=True)
    cnt_ref[...] = carry_ref[...]


def _mix(o_sb, o_g, p32, x_all, gnorm, w_out, nffn, w_r, b_r, n_tok):
    tm = _pick_tile(n_tok, (256, 128))
    g_blk = (3 * SB_WIDTH + 2 * GLA_KEY_WIDTH + GLA_WIDTH) // GLA_WIDTH
    ltri = jnp.asarray(np.tril(np.ones((tm, tm), np.float32), -1), BF16)
    const = lambda shape: pl.BlockSpec(shape, lambda i: (0, 0))
    tile = lambda w: pl.BlockSpec((tm, w), lambda i: (i, 0))
    return pl.pallas_call(
        _mix_kernel,
        grid=(n_tok // tm,),
        in_specs=[tile(SB_WIDTH), tile(GLA_WIDTH),
                  pl.BlockSpec((tm, GLA_WIDTH), lambda i: (i, g_blk)),
                  tile(D_MODEL), const((1, GLA_DV)), const((D_MODEL, D_MODEL)), const((1, D_MODEL)),
                  const((D_MODEL, LANES)), const((1, LANES)), const((tm, tm))],
        out_specs=[tile(D_MODEL), tile(D_MODEL // 2), tile(LANES), tile(LANES), tile(LANES),
                   const((1, LANES))],
        out_shape=[jax.ShapeDtypeStruct((n_tok, D_MODEL), F32),
                   jax.ShapeDtypeStruct((n_tok, D_MODEL // 2), jnp.uint32),
                   jax.ShapeDtypeStruct((n_tok, LANES), jnp.int32),
                   jax.ShapeDtypeStruct((n_tok, LANES), F32),
                   jax.ShapeDtypeStruct((n_tok, LANES), jnp.int32),
                   jax.ShapeDtypeStruct((1, LANES), F32)],
        scratch_shapes=[pltpu.VMEM((1, LANES), F32)],
        compiler_params=_cparams(("arbitrary",)),
    )(o_sb, o_g, p32, x_all, gnorm, w_out, nffn, w_r, b_r, ltri)


def _dispatch_kernel(dest_ref, x_ref, xs_ref, sem, *, rows):
    def row_copy(r, k):
        return pltpu.make_async_copy(x_ref.at[pl.ds(r, 1), :],
                                     xs_ref.at[pl.ds(dest_ref[0, 0, r * TOP_K + k], 1), :], sem)

    def start(r, carry):
        for k in range(TOP_K):
            row_copy(r, k).start()
        return carry

    def wait(r, carry):
        for k in range(TOP_K):
            row_copy(r, k).wait()
        return carry

    lax.fori_loop(0, rows, start, 0)
    lax.fori_loop(0, rows, wait, 0)


def _dispatch(xp, dest, n_slots):
    n_tok, width = xp.shape
    rows = _pick_tile(n_tok, (256, 128))
    dest3 = dest.reshape(n_tok // rows, 1, rows * TOP_K)
    return pl.pallas_call(
        functools.partial(_dispatch_kernel, rows=rows),
        grid=(n_tok // rows,),
        in_specs=[pl.BlockSpec((1, 1, rows * TOP_K), lambda i: (i, 0, 0), memory_space=pltpu.SMEM),
                  pl.BlockSpec((rows, width), lambda i: (i, 0))],
        out_specs=pl.BlockSpec(memory_space=pl.ANY),
        out_shape=jax.ShapeDtypeStruct((n_slots, width), xp.dtype),
        scratch_shapes=[pltpu.SemaphoreType.DMA(())],
        compiler_params=_cparams(("arbitrary",)),
    )(dest3, xp)


def _unpack_bf16_pairs(packed):
    lo = pltpu.bitcast(packed << 16, F32).astype(BF16)
    hi = pltpu.bitcast(packed & jnp.uint32(0xFFFF0000), F32).astype(BF16)
    return jnp.concatenate([lo, hi], axis=1)


def _expert_changed(be_ref, m):
    prev = be_ref[jnp.maximum(m - 1, 0)]
    return (m == 0) | (be_ref[m] != prev)


def _moe_up_kernel(be_ref, nu_ref, xs_ref, wg_ref, wu_ref, bg_ref, bu_ref, h_ref, w16_ref, *, tf):
    m = pl.program_id(1)

    @pl.when(_expert_changed(be_ref, m))
    def _():
        w16_ref[:, :tf] = wg_ref[...].astype(BF16)
        w16_ref[:, tf:] = wu_ref[...].astype(BF16)

    @pl.when(m < nu_ref[0])
    def _():
        x = _unpack_bf16_pairs(xs_ref[...])
        gu = jnp.dot(x, w16_ref[...], preferred_element_type=F32)
        g = jnp.minimum(gu[:, :tf] + bg_ref[...], SWIGLU_LIMIT)
        u = jnp.clip(gu[:, tf:] + bu_ref[...], -SWIGLU_LIMIT, SWIGLU_LIMIT)
        h_ref[...] = ((u + 1.0) * (g * jax.nn.sigmoid(g * SWIGLU_ALPHA))).astype(h_ref.dtype)


def _moe_down_kernel(be_ref, nu_ref, h_ref, wd_ref, bd_ref, y_ref, w16_ref):
    m = pl.program_id(1)

    @pl.when(_expert_changed(be_ref, m))
    def _():
        w16_ref[...] = wd_ref[...].astype(BF16)

    @pl.when(m < nu_ref[0])
    def _():
        y_ref[...] = jnp.dot(h_ref[...], w16_ref[...], preferred_element_type=F32) + bd_ref[...]


def _moe_experts(xs, block_expert, n_used, w_gate_up, b_gate_up, w_down, b_down):
    n_slots = xs.shape[0]
    n_blocks = n_slots // MOE_TM
    tf = 512
    n_f = D_FF // tf
    live = lambda m, nu: jnp.minimum(m, nu[0] - 1)
    h = pl.pallas_call(
        functools.partial(_moe_up_kernel, tf=tf),
        grid_spec=pltpu.PrefetchScalarGridSpec(
            num_scalar_prefetch=2,
            grid=(n_f, n_blocks),
            in_specs=[
                pl.BlockSpec((MOE_TM, D_MODEL // 2), lambda j, m, be, nu: (live(m, nu), 0)),
                pl.BlockSpec((None, D_MODEL, tf), lambda j, m, be, nu: (be[m], 0, j)),
                pl.BlockSpec((None, D_MODEL, tf), lambda j, m, be, nu: (be[m], 0, n_f + j)),
                pl.BlockSpec((None, 1, tf), lambda j, m, be, nu: (be[m], 0, j)),
                pl.BlockSpec((None, 1, tf), lambda j, m, be, nu: (be[m], 0, n_f + j)),
            ],
            out_specs=pl.BlockSpec((MOE_TM, tf), lambda j, m, be, nu: (live(m, nu), j)),
            scratch_shapes=[pltpu.VMEM((D_MODEL, 2 * tf), BF16)],
        ),
        out_shape=jax.ShapeDtypeStruct((n_slots, D_FF), BF16),
        compiler_params=_cparams(("arbitrary", "arbitrary")),
    )(block_expert, n_used, xs, w_gate_up, w_gate_up, b_gate_up, b_gate_up)

    tn = 1024
    return pl.pallas_call(
        _moe_down_kernel,
        grid_spec=pltpu.PrefetchScalarGridSpec(
            num_scalar_prefetch=2,
            grid=(D_MODEL // tn, n_blocks),
            in_specs=[
                pl.BlockSpec((MOE_TM, D_FF), lambda j, m, be, nu: (live(m, nu), 0)),
                pl.BlockSpec((None, D_FF, tn), lambda j, m, be, nu: (be[m], 0, j)),
                pl.BlockSpec((None, 1, tn), lambda j, m, be, nu: (be[m], 0, j)),
            ],
            out_specs=pl.BlockSpec((MOE_TM, tn), lambda j, m, be, nu: (live(m, nu), j)),
            scratch_shapes=[pltpu.VMEM((D_FF, tn), BF16)],
        ),
        out_shape=jax.ShapeDtypeStruct((n_slots, D_MODEL), F32),
        compiler_params=_cparams(("arbitrary", "arbitrary")),
    )(block_expert, n_used, h, w_down, b_down)


def _combine_kernel(dest_ref, y_ref, gate_ref, h_ref, nf_ref, out_ref, buf_ref, sem, *, rows):
    def row_copy(r, k):
        return pltpu.make_async_copy(y_ref.at[pl.ds(dest_ref[0, 0, r * TOP_K + k], 1), :],
                                     buf_ref.at[k, pl.ds(r, 1), :], sem)

    def start(r, carry):
        for k in range(TOP_K):
            row_copy(r, k).start()
        return carry

    def wait(r, carry):
        for k in range(TOP_K):
            row_copy(r, k).wait()
        return carry

    lax.fori_loop(0, rows, start, 0)
    lax.fori_loop(0, rows, wait, 0)

    gate = gate_ref[...]
    moe = buf_ref[0] * gate[:, 0:1]
    for k in range(1, TOP_K):
        moe = moe + buf_ref[k] * gate[:, k:k + 1]
    h = h_ref[...] + moe
    inv = lax.rsqrt(jnp.mean(h * h, axis=-1, keepdims=True) + RMS_EPS)
    out_ref[...] = (h * inv) * nf_ref[...]


def _combine(y_slots, dest, gate, h, norm_final):
    n_tok = h.shape[0]
    rows = _pick_tile(n_tok, (256, 128))
    dest3 = dest.reshape(n_tok // rows, 1, rows * TOP_K)
    return pl.pallas_call(
        functools.partial(_combine_kernel, rows=rows),
        grid=(n_tok // rows,),
        in_specs=[pl.BlockSpec((1, 1, rows * TOP_K), lambda i: (i, 0, 0), memory_space=pltpu.SMEM),
                  pl.BlockSpec(memory_space=pl.ANY),
                  pl.BlockSpec((rows, LANES), lambda i: (i, 0)),
                  pl.BlockSpec((rows, D_MODEL), lambda i: (i, 0)),
                  pl.BlockSpec((1, D_MODEL), lambda i: (0, 0))],
        out_specs=pl.BlockSpec((rows, D_MODEL), lambda i: (i, 0)),
        out_shape=jax.ShapeDtypeStruct((n_tok, D_MODEL), F32),
        scratch_shapes=[pltpu.VMEM((TOP_K, rows, D_MODEL), F32), pltpu.SemaphoreType.DMA(())],
        compiler_params=_cparams(("arbitrary",)),
    )(dest3, y_slots, gate, h, norm_final)


def kernel(x_prompt, x_sample, cache_sb_k, cache_sb_v, state_gla, meta_tokens, norm_mix, w_in, w_gk2, b_gk,
           gla_norm, w_out, norm_ffn, w_router, b_router, w_gate_up, b_gate_up, w_down, b_down, norm_final):
    assert x_prompt.shape[0] == 1 and norm_mix.shape[0] == 1, "one prompt stream, one layer"
    n_prompt = x_prompt.shape[1]
    n_batch, n_new = x_sample.shape[:2]
    n_past = cache_sb_k.shape[2]
    assert n_new == GLA_CHUNK and n_prompt % SB_TQ == 0 and n_past % SB_TK == 0
    n_tok = n_prompt + n_batch * n_new
    assert n_tok % META_BLOCK == 0

    x_all = jnp.concatenate([
        x_prompt[0], x_sample.reshape(n_batch * n_new, D_MODEL), meta_tokens.astype(F32),
        jnp.zeros((META_BLOCK - N_META, D_MODEL), F32)], axis=0)

    w_in0 = w_in[0]
    w_main = w_in0[:, :D_MAIN].astype(BF16)
    w_a = jnp.pad(w_in0[:, D_MAIN:], ((0, 0), (0, LANES - GLA_RANK))).astype(BF16)
    w_gk = jnp.pad(w_gk2[0], ((0, LANES - GLA_RANK), (0, 0)))
    colscale = np.ones((1, D_MAIN), np.float32)
    colscale[0, :SB_WIDTH] = SB_DIM ** -0.5
    colscale[0, 3 * SB_WIDTH:3 * SB_WIDTH + GLA_KEY_WIDTH] = GLA_DK ** -0.5
    p32, p16, lg = _inproj(x_all, norm_mix, w_main, jnp.asarray(colscale), w_a, w_gk, b_gk)

    u = jnp.asarray(np.tril(np.ones((SB_TK, SB_TK), np.float32)), BF16)
    o_sb_p = _sb_prompt(p16, u, n_prompt, n_tok)
    o_sb_s = _sb_sample(p16, cache_sb_k[0].reshape(n_batch, n_past, SB_WIDTH),
                        cache_sb_v[0].reshape(n_batch, n_past, SB_WIDTH), u, n_prompt, n_batch, n_new)
    o_sb = jnp.concatenate([o_sb_p, o_sb_s], axis=0)

    tri, sel = _gla_constants()
    dims = dict(n_prompt=n_prompt, n_tok=n_tok, n_batch=n_batch)
    o_g_p, state_p = _gla(p32, lg, None, tri, sel, prompt=True, **dims)
    o_g_s, state_s = _gla(p32, lg, state_gla[0].astype(F32), tri, sel, prompt=False, **dims)
    o_g = jnp.concatenate([o_g_p, o_g_s], axis=0)

    w_r = jnp.pad(w_router[0].astype(F32), ((0, 0), (0, LANES - N_EXPERTS)))
    b_r = jnp.pad(b_router.astype(F32), ((0, 0), (0, LANES - N_EXPERTS)), constant_values=-jnp.inf)
    h, xp, top_idx, gate, rank, counts = _mix(o_sb, o_g, p32, x_all, gla_norm, w_out[0].astype(BF16),
                                              norm_ffn, w_r, b_r, n_tok)

    counts = counts[0, :N_EXPERTS].astype(jnp.int32)
    padded = (counts + MOE_TM - 1) // MOE_TM * MOE_TM
    ends = jnp.cumsum(padded)
    n_blocks = -(-n_tok * TOP_K // MOE_TM) + N_EXPERTS
    dest = ((ends - padded)[top_idx[:, :TOP_K]] + rank[:, :TOP_K]).astype(jnp.int32)
    block_expert = jnp.minimum(
        jnp.searchsorted(ends, jnp.arange(n_blocks, dtype=jnp.int32) * MOE_TM, side='right'),
        N_EXPERTS - 1).astype(jnp.int32)
    n_used = (ends[-1:] // MOE_TM).astype(jnp.int32)

    xs = _dispatch(xp, dest, n_blocks * MOE_TM)
    y_slots = _moe_experts(xs, block_expert, n_used, w_gate_up[0], b_gate_up[0][:, None, :],
                           w_down[0], b_down[0][:, None, :])
    y_all = _combine(y_slots, dest, gate, h, norm_final[None, :])

    y_prompt = y_all[:n_prompt][None]
    y_sample = y_all[n_prompt:].reshape(n_batch, n_new, D_MODEL)
    k_lo, v_lo = SB_WIDTH, 2 * SB_WIDTH

    def kv_prompt(lo):
        rows = jnp.concatenate([p32[n_tok:n_tok + N_META, lo:lo + SB_WIDTH], p32[:n_prompt, lo:lo + SB_WIDTH]])
        return rows.reshape(1, 1, N_META + n_prompt, SB_HEADS, SB_DIM)

    def kv_sample(lo):
        return p32[n_prompt:n_tok, lo:lo + SB_WIDTH].reshape(1, n_batch, n_new, SB_HEADS, SB_DIM)

    return (y_prompt, y_sample, kv_prompt(k_lo), kv_prompt(v_lo), state_p[None],
            kv_sample(k_lo), kv_sample(v_lo), state_s[None])
```

```python
import functools

import jax
import jax.numpy as jnp
import numpy as np
from jax import lax
from jax.experimental import pallas as pl
from jax.experimental.pallas import tpu as pltpu

F32 = jnp.float32
BF16 = jnp.bfloat16
HIGHEST = lax.Precision.HIGHEST

D_MODEL = 2048
N_META = 16
META_BLOCK = 128
SB_HEADS = 8
SB_DIM = 128
SB_WIDTH = SB_HEADS * SB_DIM
GLA_HEADS = 4
GLA_DK = 128
GLA_DV = 256
GLA_KEY_WIDTH = GLA_HEADS * GLA_DK
GLA_WIDTH = GLA_HEADS * GLA_DV
GLA_RANK = 16
GLA_TAU = 16.0
GLA_CHUNK = 64
N_EXPERTS = 32
TOP_K = 4
D_FF = 2048
SWIGLU_LIMIT = 7.0
SWIGLU_ALPHA = 1.702
RMS_EPS = 1e-5
D_MAIN = 3 * SB_WIDTH + 2 * GLA_KEY_WIDTH + 2 * GLA_WIDTH
LANES = 128
LOG2E = 1.4426950408889634

SB_TQ = 512
SB_TK = 256
MOE_TM = 512
VMEM_LIMIT = 56 * 1024 * 1024


def _cparams(sem):
    return pltpu.CompilerParams(dimension_semantics=sem, vmem_limit_bytes=VMEM_LIMIT)


def _pick_tile(n, cands):
    for c in cands:
        if n % c == 0:
            return c
    raise ValueError(f"no tile for {n} in {cands}")


def _softplus(z):
    return jnp.maximum(z, 0.0) + jnp.log(1.0 + jnp.exp(-jnp.abs(z)))


def _inproj_kernel(x_ref, gain_ref, w_ref, scale_ref, wa_ref, wgk_ref, bgk_ref,
                   p32_ref, p16_ref, lg_ref, xn_ref):
    @pl.when(pl.program_id(1) == 0)
    def _():
        x = x_ref[...]
        inv = lax.rsqrt(jnp.mean(x * x, axis=-1, keepdims=True) + RMS_EPS)
        xn = ((x * inv) * gain_ref[...]).astype(BF16)
        xn_ref[...] = xn
        a = jnp.dot(xn, wa_ref[...], preferred_element_type=F32)
        pre = jnp.dot(a, wgk_ref[...], preferred_element_type=F32, precision=HIGHEST) + bgk_ref[...]
        lg_ref[...] = -_softplus(-pre) * (1.0 / GLA_TAU)

    y = jnp.dot(xn_ref[...], w_ref[...], preferred_element_type=F32) * scale_ref[...]
    p32_ref[...] = y
    p16_ref[...] = y.astype(BF16)


def _inproj(x_all, gain, w_main, colscale, w_a, w_gk, b_gk):
    rows = x_all.shape[0]
    tm = _pick_tile(rows, (640, 512, 256, 128))
    tn = 1024
    return pl.pallas_call(
        _inproj_kernel,
        grid=(rows // tm, D_MAIN // tn),
        in_specs=[
            pl.BlockSpec((tm, D_MODEL), lambda i, j: (i, 0)),
            pl.BlockSpec((1, D_MODEL), lambda i, j: (0, 0)),
            pl.BlockSpec((D_MODEL, tn), lambda i, j: (0, j)),
            pl.BlockSpec((1, tn), lambda i, j: (0, j)),
            pl.BlockSpec((D_MODEL, LANES), lambda i, j: (0, 0)),
            pl.BlockSpec((LANES, GLA_KEY_WIDTH), lambda i, j: (0, 0)),
            pl.BlockSpec((1, GLA_KEY_WIDTH), lambda i, j: (0, 0)),
        ],
        out_specs=[
            pl.BlockSpec((tm, tn), lambda i, j: (i, j)),
            pl.BlockSpec((tm, tn), lambda i, j: (i, j)),
            pl.BlockSpec((tm, GLA_KEY_WIDTH), lambda i, j: (i, 0)),
        ],
        out_shape=[
            jax.ShapeDtypeStruct((rows, D_MAIN), F32),
            jax.ShapeDtypeStruct((rows, D_MAIN), BF16),
            jax.ShapeDtypeStruct((rows, GLA_KEY_WIDTH), F32),
        ],
        scratch_shapes=[pltpu.VMEM((tm, D_MODEL), BF16)],
        compiler_params=_cparams(("arbitrary", "arbitrary")),
    )(x_all, gain, w_main, colscale, w_a, w_gk, b_gk)


def _sb_softplus2(z, allowed):
    neg_abs = pltpu.bitcast(pltpu.bitcast(z, jnp.uint32) | jnp.uint32(0x80000000), F32)
    sp = jnp.maximum(z, 0.0) + jnp.log2(1.0 + jnp.exp2(neg_abs))
    if allowed is not None:
        sp = jnp.where(allowed, sp, 0.0)
    return sp


def _sb_mass(z, u, allowed):
    return jnp.dot(_sb_softplus2(z, allowed).astype(BF16), u, preferred_element_type=F32)


def _sb_weight(z, r, c, allowed):
    w = jnp.exp2(z - r - c)
    if allowed is not None:
        w = jnp.where(allowed, w, 0.0)
    return w.astype(BF16)


def _sb_visit(q, kblk, vblk, u, c, alloweds):
    tk = u.shape[0]
    z = lax.dot_general(q, kblk, (((1,), (1,)), ((), ())), preferred_element_type=F32)
    n = len(alloweds)
    zs = [z[:, s * tk:(s + 1) * tk] for s in range(n)]
    rs = [_sb_mass(zs[s], u, alloweds[s]) for s in range(n)]
    ws = [None] * n
    for s in reversed(range(n)):
        ws[s] = _sb_weight(zs[s], rs[s], c, alloweds[s])
        c = c + rs[s][:, 0:1]
    w = ws[0] if n == 1 else jnp.concatenate(ws, axis=1)
    return jnp.dot(w, vblk, preferred_element_type=F32), c


def _sb_prompt_kernel(q_ref, k_ref, v_ref, km_ref, vm_ref, u_ref, o_ref,
                      acc_ref, c_ref, z_ref, sp_ref, t_ref, tot_ref):
    i = pl.program_id(1)
    q = q_ref[pl.ds(pl.multiple_of(i * SB_TQ, SB_TQ), SB_TQ), :]
    u = u_ref[...]
    n_sub = SB_TQ // SB_TK

    def key_rows(blk):
        return pl.ds(pl.multiple_of(blk * SB_TQ, SB_TQ), SB_TQ)

    def score_matmul(blk):
        return lax.dot_general(q, k_ref[key_rows(blk), :], (((1,), (1,)), ((), ())),
                               preferred_element_type=F32)

    def stage_softplus(z, slot, alloweds):
        z_ref[slot] = z
        for s in range(n_sub):
            cols = slice(s * SB_TK, (s + 1) * SB_TK)
            sp_ref[slot, :, cols] = _sb_softplus2(z[:, cols], alloweds[s]).astype(BF16)

    def stage_mass(slot, alloweds):
        for s in range(n_sub):
            cols = slice(s * SB_TK, (s + 1) * SB_TK)
            r_s = jnp.dot(sp_ref[slot, :, cols], u, preferred_element_type=F32)
            t_s = z_ref[slot, :, cols] - r_s
            if alloweds[s] is not None:
                t_s = jnp.where(alloweds[s], t_s, -jnp.inf)
            t_ref[:, cols] = t_s
            tot_ref[s] = r_s[:, 0:1]

    def stage_accumulate(blk):
        live = blk >= 0
        c = c_ref[...]
        c_eff = c + jnp.where(live, 0.0, jnp.inf)
        ws = [None] * n_sub
        for s in reversed(range(n_sub)):
            ws[s] = jnp.exp2(t_ref[:, s * SB_TK:(s + 1) * SB_TK] - c_eff).astype(BF16)
            c_eff = c_eff + tot_ref[s]
            c = c + jnp.where(live, tot_ref[s], 0.0)
        acc_ref[...] += jnp.dot(jnp.concatenate(ws, axis=1), v_ref[key_rows(jnp.maximum(blk, 0)), :],
                                preferred_element_type=F32)
        c_ref[...] = c

    acc_ref[...] = jnp.zeros_like(acc_ref)
    c_ref[...] = jnp.zeros_like(c_ref)
    row_i = lax.broadcasted_iota(jnp.int32, (SB_TQ, SB_TK), 0)
    col_i = lax.broadcasted_iota(jnp.int32, (SB_TQ, SB_TK), 1)
    own = [col_i + s * SB_TK < row_i for s in range(n_sub)]
    free = [None] * n_sub
    stage_softplus(score_matmul(i), 0, own)
    stage_mass(0, own)
    stage_softplus(score_matmul(jnp.maximum(i - 1, 0)), 1, free)

    def step(p, slot):
        z = score_matmul(jnp.maximum(i - p - 2, 0))
        stage_accumulate(i - p)
        stage_mass(1 - slot, free)
        stage_softplus(z, slot, free)

    def body(n, carry):
        step(2 * n, 0)
        step(2 * n + 1, 1)
        return carry

    lax.fori_loop(0, (i + 2) // 2, body, 0)

    meta_cols = lax.broadcasted_iota(jnp.int32, (SB_TQ, META_BLOCK), 1)
    pv, _ = _sb_visit(q, km_ref[...], vm_ref[...], u[:META_BLOCK, :META_BLOCK], c_ref[...],
                      [meta_cols < N_META])
    o_ref[...] = (acc_ref[...] + pv).astype(o_ref.dtype)


def _sb_prompt(p16, u, n_prompt, n_tok):
    head_spec = lambda off: pl.BlockSpec((n_prompt, SB_DIM), lambda h, i: (0, off + h))
    meta_spec = lambda off: pl.BlockSpec((META_BLOCK, SB_DIM), lambda h, i: (n_tok // META_BLOCK, off + h))
    return pl.pallas_call(
        _sb_prompt_kernel,
        grid=(SB_HEADS, n_prompt // SB_TQ),
        in_specs=[head_spec(0), head_spec(SB_HEADS), head_spec(2 * SB_HEADS),
                  meta_spec(SB_HEADS), meta_spec(2 * SB_HEADS),
                  pl.BlockSpec((SB_TK, SB_TK), lambda h, i: (0, 0))],
        out_specs=pl.BlockSpec((SB_TQ, SB_DIM), lambda h, i: (i, h)),
        out_shape=jax.ShapeDtypeStruct((n_prompt, SB_WIDTH), BF16),
        scratch_shapes=[pltpu.VMEM((SB_TQ, SB_DIM), F32), pltpu.VMEM((SB_TQ, 1), F32),
                        pltpu.VMEM((2, SB_TQ, SB_TQ), F32), pltpu.VMEM((2, SB_TQ, SB_TQ), BF16),
                        pltpu.VMEM((SB_TQ, SB_TQ), F32), pltpu.VMEM((SB_TQ // SB_TK, SB_TQ, 1), F32)],
        compiler_params=_cparams(("arbitrary", "arbitrary")),
    )(p16, p16, p16, p16, p16, u)


SB_CACHE_SPLIT = 2


def _sb_sample_kernel(q_ref, kn_ref, vn_ref, kc_ref, vc_ref, u_ref, o_ref, acc_ref, c_ref, *, n_new, n_pos):
    first = pl.program_id(1) == 0

    @pl.when(first & (pl.program_id(0) == 0))
    def _():
        acc_ref[...] = jnp.zeros_like(acc_ref)
        c_ref[...] = jnp.zeros_like(c_ref)

    u = u_ref[...]
    r_i = lax.broadcasted_iota(jnp.int32, (n_new, n_new), 0)
    c_i = lax.broadcasted_iota(jnp.int32, (n_new, n_new), 1)
    zero_c = jnp.zeros((n_new, 1), F32)
    for head in range(SB_HEADS):
        cols = slice(head * SB_DIM, (head + 1) * SB_DIM)
        q = q_ref[:, cols]
        pv_new, tot_new = _sb_visit(q, kn_ref[:, cols], vn_ref[:, cols], u[:n_new, :n_new], zero_c,
                                    [c_i < r_i])
        acc = jnp.where(first, pv_new, acc_ref[head])
        c = jnp.where(first, tot_new, c_ref[head])
        kblk = kc_ref[pl.ds(head, n_pos, stride=SB_HEADS), :].astype(BF16)
        vblk = vc_ref[pl.ds(head, n_pos, stride=SB_HEADS), :].astype(BF16)
        pv, c = _sb_visit(q, kblk, vblk, u, c, [None] * (n_pos // SB_TK))
        acc = acc + pv
        acc_ref[head] = acc
        c_ref[head] = c
        o_ref[:, cols] = acc.astype(o_ref.dtype)


def _sb_sample(p16, cache_k, cache_v, u, n_prompt, n_batch, n_new):
    n_past = cache_k.shape[1] // SB_HEADS
    n_pos = n_past // SB_CACHE_SPLIT
    assert n_pos % SB_TK == 0
    row0 = n_prompt // n_new
    new_spec = lambda blk: pl.BlockSpec((n_new, SB_WIDTH), lambda b, j: (row0 + b, blk))
    cache_spec = pl.BlockSpec((None, n_pos * SB_HEADS, SB_DIM), lambda b, j: (b, SB_CACHE_SPLIT - 1 - j, 0))
    return pl.pallas_call(
        functools.partial(_sb_sample_kernel, n_new=n_new, n_pos=n_pos),
        grid=(n_batch, SB_CACHE_SPLIT),
        in_specs=[new_spec(0), new_spec(1), new_spec(2), cache_spec, cache_spec,
                  pl.BlockSpec((SB_TK, SB_TK), lambda b, j: (0, 0))],
        out_specs=pl.BlockSpec((n_new, SB_WIDTH), lambda b, j: (b, 0)),
        out_shape=jax.ShapeDtypeStruct((n_batch * n_new, SB_WIDTH), BF16),
        scratch_shapes=[pltpu.VMEM((SB_HEADS, n_new, SB_DIM), F32), pltpu.VMEM((SB_HEADS, n_new, 1), F32)],
        compiler_params=_cparams(("arbitrary", "arbitrary")),
    )(p16, p16, p16, cache_k, cache_v, u)


GLA_LEVELS = 6


def _gla_constants():
    c = GLA_CHUNK
    tri = np.tril(np.ones((c, c), np.float32))
    sel = np.zeros((GLA_LEVELS * c, c), np.float32)
    for lvl in range(GLA_LEVELS):
        n = c >> lvl
        for row in range(c):
            sel[lvl * c + row, (row // n) * n + n // 2 - 1] = 1.0
    return jnp.asarray(tri), jnp.asarray(sel)


def _gla_chunk(q, k, v, lg, st_ref, head, tri, sel):
    c = GLA_CHUNK
    b = jnp.dot(tri, lg, preferred_element_type=F32, precision=HIGHEST)
    refs = jnp.dot(sel, b, preferred_element_type=F32, precision=HIGHEST)
    b_last = b[c - 1:c, :]
    st = st_ref[head]
    o = lax.dot_general((q * jnp.exp(b)).astype(BF16), st.astype(BF16), (((1,), (1,)), ((), ())),
                        preferred_element_type=F32)

    t_row = lax.broadcasted_iota(jnp.int32, (c, 1), 0)
    t_idx = lax.broadcasted_iota(jnp.int32, (c, c), 0)
    s_idx = lax.broadcasted_iota(jnp.int32, (c, c), 1)
    qk = lambda a, bb: lax.dot_general(a.astype(BF16), bb.astype(BF16), (((1,), (1,)), ((), ())),
                                       preferred_element_type=F32)
    att = jnp.where(t_idx == s_idx, qk(q, k), 0.0)
    for lvl in range(GLA_LEVELS):
        half = (c >> lvl) // 2
        shift = GLA_LEVELS - 1 - lvl
        r = refs[lvl * c:(lvl + 1) * c, :]
        late = ((t_row >> shift) & 1) == 1
        qa = jnp.where(late, q * jnp.exp(b - r), 0.0)
        ka = jnp.where(late, 0.0, k * jnp.exp(r - b))
        pair = ((t_idx >> (shift + 1)) == (s_idx >> (shift + 1))) \
            & (((t_idx >> shift) & 1) == 1) & (((s_idx >> shift) & 1) == 0)
        att = att + jnp.where(pair, qk(qa, ka), 0.0)
    o = o + jnp.dot(att.astype(BF16), v.astype(BF16), preferred_element_type=F32)

    kd = (k * jnp.exp(b_last - b)).astype(BF16)
    upd = lax.dot_general(v.astype(BF16), kd, (((0,), (0,)), ((), ())), preferred_element_type=F32)
    st_ref[head] = st * jnp.exp(b_last) + upd
    return o


def _gla_kernel(*refs, prompt):
    if prompt:
        q_ref, k_ref, v_ref, lg_ref, tri_ref, sel_ref, o_ref, s_out_ref, st_ref = refs
    else:
        q_ref, k_ref, v_ref, lg_ref, s_in_ref, tri_ref, sel_ref, o_ref, s_out_ref, st_ref = refs
    step = pl.program_id(0)
    tri = tri_ref[...]
    sel = sel_ref[...]
    if prompt:
        @pl.when(step == 0)
        def _():
            st_ref[...] = jnp.zeros_like(st_ref)
        n_valid = jnp.where(step == 0, N_META, GLA_CHUNK)
        live = lax.broadcasted_iota(jnp.int32, (GLA_CHUNK, 1), 0) < n_valid
    else:
        for head in range(GLA_HEADS):
            st_ref[head] = s_in_ref[head].T
        live = None

    for head in range(GLA_HEADS):
        ks = slice(head * GLA_DK, (head + 1) * GLA_DK)
        vs = slice(head * GLA_DV, (head + 1) * GLA_DV)
        k = k_ref[:, ks]
        lg = lg_ref[:, ks]
        if live is not None:
            k = jnp.where(live, k, 0.0)
            lg = jnp.where(live, lg, 0.0)
        o_ref[:, vs] = _gla_chunk(q_ref[:, ks], k, v_ref[:, vs], lg, st_ref, head, tri, sel)

    def write_state():
        for head in range(GLA_HEADS):
            s_out_ref[head] = st_ref[head].T

    if prompt:
        pl.when(step == pl.num_programs(0) - 1)(write_state)
    else:
        write_state()


def _gla(p32, lg, state_in, tri, sel, *, prompt, n_prompt, n_tok, n_batch):
    c = GLA_CHUNK
    q_blk = 3 * SB_WIDTH // GLA_KEY_WIDTH
    v_blk = (3 * SB_WIDTH + 2 * GLA_KEY_WIDTH) // GLA_WIDTH
    if prompt:
        steps = n_prompt // c + 1
        row = lambda s: jnp.where(s == 0, n_tok // c, s - 1)
        out_row = lambda s: jnp.maximum(s - 1, 0)
        out_rows = n_prompt
        state_idx = lambda s: (0, 0, 0, 0)
        n_state = 1
    else:
        steps = n_batch
        row = lambda s: n_prompt // c + s
        out_row = lambda s: s
        out_rows = n_batch * c
        state_idx = lambda s: (s, 0, 0, 0)
        n_state = n_batch
    state_spec = pl.BlockSpec((None, GLA_HEADS, GLA_DK, GLA_DV), state_idx)
    in_specs = [
        pl.BlockSpec((c, GLA_KEY_WIDTH), lambda s: (row(s), q_blk)),
        pl.BlockSpec((c, GLA_KEY_WIDTH), lambda s: (row(s), q_blk + 1)),
        pl.BlockSpec((c, GLA_WIDTH), lambda s: (row(s), v_blk)),
        pl.BlockSpec((c, GLA_KEY_WIDTH), lambda s: (row(s), 0)),
    ]
    args = [p32, p32, p32, lg]
    if not prompt:
        in_specs.append(state_spec)
        args.append(state_in)
    in_specs += [pl.BlockSpec(tri.shape, lambda s: (0, 0)), pl.BlockSpec(sel.shape, lambda s: (0, 0))]
    args += [tri, sel]
    return pl.pallas_call(
        functools.partial(_gla_kernel, prompt=prompt),
        grid=(steps,),
        in_specs=in_specs,
        out_specs=[pl.BlockSpec((c, GLA_WIDTH), lambda s: (out_row(s), 0)), state_spec],
        out_shape=[jax.ShapeDtypeStruct((out_rows, GLA_WIDTH), F32),
                   jax.ShapeDtypeStruct((n_state, GLA_HEADS, GLA_DK, GLA_DV), F32)],
        scratch_shapes=[pltpu.VMEM((GLA_HEADS, GLA_DV, GLA_DK), F32)],
        compiler_params=_cparams(("arbitrary",)),
    )(*args)


def _mix_kernel(osb_ref, og_ref, gg_ref, x_ref, gnorm_ref, wout_ref, nffn_ref, wr_ref, br_ref, ltri_ref,
                h_ref, xp_ref, idx_ref, gate_ref, rank_ref, cnt_ref, carry_ref):
    step = pl.program_id(0)

    @pl.when(step == 0)
    def _():
        carry_ref[...] = jnp.zeros_like(carry_ref)

    parts = []
    for head in range(GLA_HEADS):
        vs = slice(head * GLA_DV, (head + 1) * GLA_DV)
        og = og_ref[:, vs]
        inv = lax.rsqrt(jnp.mean(og * og, axis=-1, keepdims=True) + RMS_EPS)
        g = gg_ref[:, vs]
        parts.append((((og * inv) * gnorm_ref[...]) * (g * jax.nn.sigmoid(g))).astype(BF16))
    mixed = jnp.dot(osb_ref[...], wout_ref[:SB_WIDTH, :], preferred_element_type=F32)
    for head in range(GLA_HEADS):
        lo = SB_WIDTH + head * GLA_DV
        mixed = mixed + jnp.dot(parts[head], wout_ref[lo:lo + GLA_DV, :], preferred_element_type=F32)
    h = x_ref[...] + mixed
    h_ref[...] = h

    inv = lax.rsqrt(jnp.mean(h * h, axis=-1, keepdims=True) + RMS_EPS)
    xn = (h * inv) * nffn_ref[...]
    half = D_MODEL // 2
    lo_bits = pltpu.bitcast(xn[:, :half].astype(BF16).astype(F32), jnp.uint32)
    hi_bits = pltpu.bitcast(xn[:, half:].astype(BF16).astype(F32), jnp.uint32)
    xp_ref[...] = (hi_bits & jnp.uint32(0xFFFF0000)) | (lo_bits >> 16)

    logits = jnp.dot(xn, wr_ref[...], preferred_element_type=F32, precision=HIGHEST) + br_ref[...]
    tm = logits.shape[0]
    lane = lax.broadcasted_iota(jnp.int32, (tm, LANES), 1)
    vals, sels = [], []
    idx_out = jnp.zeros((tm, LANES), jnp.int32)
    for k in range(TOP_K):
        m = jnp.max(logits, axis=-1, keepdims=True)
        idx = jnp.min(jnp.where(logits == m, lane, LANES), axis=-1, keepdims=True)
        sel = lane == idx
        vals.append(m)
        sels.append(sel)
        idx_out = jnp.where(lane == k, idx, idx_out)
        logits = jnp.where(sel, -jnp.inf, logits)
    exps = [jnp.exp(v - vals[0]) for v in vals]
    denom = exps[0] + exps[1] + exps[2] + exps[3]
    gate_out = jnp.zeros((tm, LANES), F32)
    for k in range(TOP_K):
        gate_out = jnp.where(lane == k, exps[k] / denom, gate_out)
    idx_ref[...] = idx_out
    gate_ref[...] = gate_out

    hot = jnp.zeros((tm, LANES), F32)
    for sel in sels:
        hot = jnp.where(sel, 1.0, hot)
    before = jnp.dot(ltri_ref[...], hot.astype(BF16), preferred_element_type=F32) + carry_ref[...]
    rank_out = jnp.zeros((tm, LANES), jnp.int32)
    for k in range(TOP_K):
        rk = jnp.sum(jnp.where(sels[k], before, 0.0), axis=-1, keepdims=True)
        rank_out = jnp.where(lane == k, rk.astype(jnp.int32), rank_out)
    rank_ref[...] = rank_out
    carry_ref[...] = carry_ref[...] + jnp.sum(hot, axis=0, keepdims=True)
    cnt_ref[...] = carry_ref[...]


def _mix(o_sb, o_g, p32, x_all, gnorm, w_out, nffn, w_r, b_r, n_tok):
    tm = _pick_tile(n_tok, (512, 256, 128))
    g_blk = (3 * SB_WIDTH + 2 * GLA_KEY_WIDTH + GLA_WIDTH) // GLA_WIDTH
    ltri = jnp.asarray(np.tril(np.ones((tm, tm), np.float32), -1), BF16)
    const = lambda shape: pl.BlockSpec(shape, lambda i: (0, 0))
    tile = lambda w: pl.BlockSpec((tm, w), lambda i: (i, 0))
    return pl.pallas_call(
        _mix_kernel,
        grid=(n_tok // tm,),
        in_specs=[tile(SB_WIDTH), tile(GLA_WIDTH),
                  pl.BlockSpec((tm, GLA_WIDTH), lambda i: (i, g_blk)),
                  tile(D_MODEL), const((1, GLA_DV)), const((D_MODEL, D_MODEL)), const((1, D_MODEL)),
                  const((D_MODEL, LANES)), const((1, LANES)), const((tm, tm))],
        out_specs=[tile(D_MODEL), tile(D_MODEL // 2), tile(LANES), tile(LANES), tile(LANES),
                   const((1, LANES))],
        out_shape=[jax.ShapeDtypeStruct((n_tok, D_MODEL), F32),
                   jax.ShapeDtypeStruct((n_tok, D_MODEL // 2), jnp.uint32),
                   jax.ShapeDtypeStruct((n_tok, LANES), jnp.int32),
                   jax.ShapeDtypeStruct((n_tok, LANES), F32),
                   jax.ShapeDtypeStruct((n_tok, LANES), jnp.int32),
                   jax.ShapeDtypeStruct((1, LANES), F32)],
        scratch_shapes=[pltpu.VMEM((1, LANES), F32)],
        compiler_params=_cparams(("arbitrary",)),
    )(o_sb, o_g, p32, x_all, gnorm, w_out, nffn, w_r, b_r, ltri)


def _dispatch_kernel(dest_ref, x_ref, xs_ref, sem, *, rows):
    def row_copy(r, k):
        return pltpu.make_async_copy(x_ref.at[pl.ds(r, 1), :],
                                     xs_ref.at[pl.ds(dest_ref[0, 0, r * TOP_K + k], 1), :], sem)

    def start(r, carry):
        for k in range(TOP_K):
            row_copy(r, k).start()
        return carry

    def wait(r, carry):
        for k in range(TOP_K):
            row_copy(r, k).wait()
        return carry

    lax.fori_loop(0, rows, start, 0)
    lax.fori_loop(0, rows, wait, 0)


def _dispatch(xp, dest, n_slots):
    n_tok, width = xp.shape
    rows = _pick_tile(n_tok, (256, 128))
    dest3 = dest.reshape(n_tok // rows, 1, rows * TOP_K)
    return pl.pallas_call(
        functools.partial(_dispatch_kernel, rows=rows),
        grid=(n_tok // rows,),
        in_specs=[pl.BlockSpec((1, 1, rows * TOP_K), lambda i: (i, 0, 0), memory_space=pltpu.SMEM),
                  pl.BlockSpec((rows, width), lambda i: (i, 0))],
        out_specs=pl.BlockSpec(memory_space=pl.ANY),
        out_shape=jax.ShapeDtypeStruct((n_slots, width), xp.dtype),
        scratch_shapes=[pltpu.SemaphoreType.DMA(())],
        compiler_params=_cparams(("arbitrary",)),
    )(dest3, xp)


def _unpack_bf16_pairs(packed):
    lo = pltpu.bitcast(packed << 16, F32).astype(BF16)
    hi = pltpu.bitcast(packed & jnp.uint32(0xFFFF0000), F32).astype(BF16)
    return jnp.concatenate([lo, hi], axis=1)


def _expert_changed(be_ref, m):
    prev = be_ref[jnp.maximum(m - 1, 0)]
    return (m == 0) | (be_ref[m] != prev)


def _moe_up_kernel(be_ref, nu_ref, xs_ref, wg_ref, wu_ref, bg_ref, bu_ref, h_ref, w16_ref, *, tf):
    m = pl.program_id(1)

    @pl.when(_expert_changed(be_ref, m))
    def _():
        w16_ref[:, :tf] = wg_ref[...].astype(BF16)
        w16_ref[:, tf:] = wu_ref[...].astype(BF16)

    @pl.when(m < nu_ref[0])
    def _():
        x = _unpack_bf16_pairs(xs_ref[...])
        gu = jnp.dot(x, w16_ref[...], preferred_element_type=F32)
        g = jnp.minimum(gu[:, :tf] + bg_ref[...], SWIGLU_LIMIT)
        u = jnp.clip(gu[:, tf:] + bu_ref[...], -SWIGLU_LIMIT, SWIGLU_LIMIT)
        h_ref[...] = ((u + 1.0) * (g * jax.nn.sigmoid(g * SWIGLU_ALPHA))).astype(h_ref.dtype)


def _moe_down_kernel(be_ref, nu_ref, h_ref, wd_ref, bd_ref, y_ref, w16_ref):
    m = pl.program_id(1)

    @pl.when(_expert_changed(be_ref, m))
    def _():
        w16_ref[...] = wd_ref[...].astype(BF16)

    @pl.when(m < nu_ref[0])
    def _():
        y_ref[...] = jnp.dot(h_ref[...], w16_ref[...], preferred_element_type=F32) + bd_ref[...]


def _moe_experts(xs, block_expert, n_used, w_gate_up, b_gate_up, w_down, b_down):
    n_slots = xs.shape[0]
    n_blocks = n_slots // MOE_TM
    tf = 512
    n_f = D_FF // tf
    live = lambda m, nu: jnp.minimum(m, nu[0] - 1)
    h = pl.pallas_call(
        functools.partial(_moe_up_kernel, tf=tf),
        grid_spec=pltpu.PrefetchScalarGridSpec(
            num_scalar_prefetch=2,
            grid=(n_f, n_blocks),
            in_specs=[
                pl.BlockSpec((MOE_TM, D_MODEL // 2), lambda j, m, be, nu: (live(m, nu), 0)),
                pl.BlockSpec((None, D_MODEL, tf), lambda j, m, be, nu: (be[m], 0, j)),
                pl.BlockSpec((None, D_MODEL, tf), lambda j, m, be, nu: (be[m], 0, n_f + j)),
                pl.BlockSpec((None, 1, tf), lambda j, m, be, nu: (be[m], 0, j)),
                pl.BlockSpec((None, 1, tf), lambda j, m, be, nu: (be[m], 0, n_f + j)),
            ],
            out_specs=pl.BlockSpec((MOE_TM, tf), lambda j, m, be, nu: (live(m, nu), j)),
            scratch_shapes=[pltpu.VMEM((D_MODEL, 2 * tf), BF16)],
        ),
        out_shape=jax.ShapeDtypeStruct((n_slots, D_FF), BF16),
        compiler_params=_cparams(("arbitrary", "arbitrary")),
    )(block_expert, n_used, xs, w_gate_up, w_gate_up, b_gate_up, b_gate_up)

    tn = 1024
    return pl.pallas_call(
        _moe_down_kernel,
        grid_spec=pltpu.PrefetchScalarGridSpec(
            num_scalar_prefetch=2,
            grid=(D_MODEL // tn, n_blocks),
            in_specs=[
                pl.BlockSpec((MOE_TM, D_FF), lambda j, m, be, nu: (live(m, nu), 0)),
                pl.BlockSpec((None, D_FF, tn), lambda j, m, be, nu: (be[m], 0, j)),
                pl.BlockSpec((None, 1, tn), lambda j, m, be, nu: (be[m], 0, j)),
            ],
            out_specs=pl.BlockSpec((MOE_TM, tn), lambda j, m, be, nu: (live(m, nu), j)),
            scratch_shapes=[pltpu.VMEM((D_FF, tn), BF16)],
        ),
        out_shape=jax.ShapeDtypeStruct((n_slots, D_MODEL), F32),
        compiler_params=_cparams(("arbitrary", "arbitrary")),
    )(block_expert, n_used, h, w_down, b_down)


def _combine_kernel(dest_ref, y_ref, gate_ref, h_ref, nf_ref, out_ref, buf_ref, sem, *, rows):
    def row_copy(r, k):
        return pltpu.make_async_copy(y_ref.at[pl.ds(dest_ref[0, 0, r * TOP_K + k], 1), :],
                                     buf_ref.at[k, pl.ds(r, 1), :], sem)

    def start(r, carry):
        for k in range(TOP_K):
            row_copy(r, k).start()
        return carry

    def wait(r, carry):
        for k in range(TOP_K):
            row_copy(r, k).wait()
        return carry

    lax.fori_loop(0, rows, start, 0)
    lax.fori_loop(0, rows, wait, 0)

    gate = gate_ref[...]
    moe = buf_ref[0] * gate[:, 0:1]
    for k in range(1, TOP_K):
        moe = moe + buf_ref[k] * gate[:, k:k + 1]
    h = h_ref[...] + moe
    inv = lax.rsqrt(jnp.mean(h * h, axis=-1, keepdims=True) + RMS_EPS)
    out_ref[...] = (h * inv) * nf_ref[...]


def _combine(y_slots, dest, gate, h, norm_final):
    n_tok = h.shape[0]
    rows = _pick_tile(n_tok, (256, 128))
    dest3 = dest.reshape(n_tok // rows, 1, rows * TOP_K)
    return pl.pallas_call(
        functools.partial(_combine_kernel, rows=rows),
        grid=(n_tok // rows,),
        in_specs=[pl.BlockSpec((1, 1, rows * TOP_K), lambda i: (i, 0, 0), memory_space=pltpu.SMEM),
                  pl.BlockSpec(memory_space=pl.ANY),
                  pl.BlockSpec((rows, LANES), lambda i: (i, 0)),
                  pl.BlockSpec((rows, D_MODEL), lambda i: (i, 0)),
                  pl.BlockSpec((1, D_MODEL), lambda i: (0, 0))],
        out_specs=pl.BlockSpec((rows, D_MODEL), lambda i: (i, 0)),
        out_shape=jax.ShapeDtypeStruct((n_tok, D_MODEL), F32),
        scratch_shapes=[pltpu.VMEM((TOP_K, rows, D_MODEL), F32), pltpu.SemaphoreType.DMA(())],
        compiler_params=_cparams(("arbitrary",)),
    )(dest3, y_slots, gate, h, norm_final)


def kernel(x_prompt, x_sample, cache_sb_k, cache_sb_v, state_gla, meta_tokens, norm_mix, w_in, w_gk2, b_gk,
           gla_norm, w_out, norm_ffn, w_router, b_router, w_gate_up, b_gate_up, w_down, b_down, norm_final):
    assert x_prompt.shape[0] == 1 and norm_mix.shape[0] == 1, "one prompt stream, one layer"
    n_prompt = x_prompt.shape[1]
    n_batch, n_new = x_sample.shape[:2]
    n_past = cache_sb_k.shape[2]
    assert n_new == GLA_CHUNK and n_prompt % SB_TQ == 0 and n_past % (SB_TK * SB_CACHE_SPLIT) == 0
    n_tok = n_prompt + n_batch * n_new
    assert n_tok % META_BLOCK == 0

    x_all = jnp.concatenate([
        x_prompt[0], x_sample.reshape(n_batch * n_new, D_MODEL), meta_tokens.astype(F32),
        jnp.zeros((META_BLOCK - N_META, D_MODEL), F32)], axis=0)

    w_in0 = w_in[0]
    w_main = w_in0[:, :D_MAIN].astype(BF16)
    w_a = jnp.pad(w_in0[:, D_MAIN:], ((0, 0), (0, LANES - GLA_RANK))).astype(BF16)
    w_gk = jnp.pad(w_gk2[0], ((0, LANES - GLA_RANK), (0, 0)))
    colscale = np.ones((1, D_MAIN), np.float32)
    colscale[0, :SB_WIDTH] = SB_DIM ** -0.5 * LOG2E
    colscale[0, 3 * SB_WIDTH:3 * SB_WIDTH + GLA_KEY_WIDTH] = GLA_DK ** -0.5
    p32, p16, lg = _inproj(x_all, norm_mix, w_main, jnp.asarray(colscale), w_a, w_gk, b_gk)

    u = jnp.asarray(np.tril(np.ones((SB_TK, SB_TK), np.float32)), BF16)
    o_sb_p = _sb_prompt(p16, u, n_prompt, n_tok)
    o_sb_s = _sb_sample(p16, cache_sb_k[0].reshape(n_batch, n_past * SB_HEADS, SB_DIM),
                        cache_sb_v[0].reshape(n_batch, n_past * SB_HEADS, SB_DIM), u, n_prompt, n_batch, n_new)
    o_sb = jnp.concatenate([o_sb_p, o_sb_s], axis=0)

    tri, sel = _gla_constants()
    dims = dict(n_prompt=n_prompt, n_tok=n_tok, n_batch=n_batch)
    o_g_p, state_p = _gla(p32, lg, None, tri, sel, prompt=True, **dims)
    o_g_s, state_s = _gla(p32, lg, state_gla[0].astype(F32), tri, sel, prompt=False, **dims)
    o_g = jnp.concatenate([o_g_p, o_g_s], axis=0)

    w_r = jnp.pad(w_router[0].astype(F32), ((0, 0), (0, LANES - N_EXPERTS)))
    b_r = jnp.pad(b_router.astype(F32), ((0, 0), (0, LANES - N_EXPERTS)), constant_values=-jnp.inf)
    h, xp, top_idx, gate, rank, counts = _mix(o_sb, o_g, p32, x_all, gla_norm, w_out[0].astype(BF16),
                                              norm_ffn, w_r, b_r, n_tok)

    counts = counts[0, :N_EXPERTS].astype(jnp.int32)
    padded = (counts + MOE_TM - 1) // MOE_TM * MOE_TM
    ends = jnp.cumsum(padded)
    n_blocks = -(-n_tok * TOP_K // MOE_TM) + N_EXPERTS
    dest = ((ends - padded)[top_idx[:, :TOP_K]] + rank[:, :TOP_K]).astype(jnp.int32)
    block_start = jnp.arange(n_blocks, dtype=jnp.int32) * MOE_TM
    block_expert = jnp.minimum(jnp.sum((ends[None, :] <= block_start[:, None]).astype(jnp.int32), axis=1),
                               N_EXPERTS - 1)
    n_used = (ends[-1:] // MOE_TM).astype(jnp.int32)

    xs = _dispatch(xp, dest, n_blocks * MOE_TM)
    y_slots = _moe_experts(xs, block_expert, n_used, w_gate_up[0], b_gate_up[0][:, None, :],
                           w_down[0], b_down[0][:, None, :])
    y_all = _combine(y_slots, dest, gate, h, norm_final[None, :])

    y_prompt = y_all[:n_prompt][None]
    y_sample = y_all[n_prompt:].reshape(n_batch, n_new, D_MODEL)
    k_lo, v_lo = SB_WIDTH, 2 * SB_WIDTH

    def kv_prompt(lo):
        rows = jnp.concatenate([p32[n_tok:n_tok + N_META, lo:lo + SB_WIDTH], p32[:n_prompt, lo:lo + SB_WIDTH]])
        return rows.reshape(1, 1, N_META + n_prompt, SB_HEADS, SB_DIM)

    def kv_sample(lo):
        return p32[n_prompt:n_tok, lo:lo + SB_WIDTH].reshape(1, n_batch, n_new, SB_HEADS, SB_DIM)

    return (y_prompt, y_sample, kv_prompt(k_lo), kv_prompt(v_lo), state_p[None],
            kv_sample(k_lo), kv_sample(v_lo), state_s[None])
```

```python
import functools
import math

import jax
import jax.numpy as jnp
import numpy as np
from jax import lax
from jax.experimental import pallas as pl
from jax.experimental.pallas import tpu as pltpu

F32 = jnp.float32
BF16 = jnp.bfloat16
HIGHEST = lax.Precision.HIGHEST

D_MODEL = 2048
N_META = 16
META_BLOCK = 128
SB_HEADS = 8
SB_DIM = 128
SB_WIDTH = SB_HEADS * SB_DIM
GLA_HEADS = 4
GLA_DK = 128
GLA_DV = 256
GLA_KEY_WIDTH = GLA_HEADS * GLA_DK
GLA_WIDTH = GLA_HEADS * GLA_DV
GLA_RANK = 16
GLA_TAU = 16.0
GLA_CHUNK = 64
N_EXPERTS = 32
TOP_K = 4
D_FF = 2048
SWIGLU_LIMIT = 7.0
SWIGLU_ALPHA = 1.702
RMS_EPS = 1e-5
D_MAIN = 3 * SB_WIDTH + 2 * GLA_KEY_WIDTH + 2 * GLA_WIDTH
LANES = 128
LOG2E = 1.4426950408889634

SB_TQ = 512
SB_TK = 256
MOE_TM = 512
VMEM_LIMIT = 56 * 1024 * 1024


def _cparams(sem):
    return pltpu.CompilerParams(dimension_semantics=sem, vmem_limit_bytes=VMEM_LIMIT)


def _pick_tile(n, cands):
    for c in cands:
        if n % c == 0:
            return c
    raise ValueError(f"no tile for {n} in {cands}")


def _softplus(z):
    return jnp.maximum(z, 0.0) + jnp.log(1.0 + jnp.exp(-jnp.abs(z)))


def _inproj_kernel(x_ref, gain_ref, w_ref, scale_ref, wa_ref, wgk_ref, bgk_ref,
                   p32_ref, p16_ref, lg_ref, xn_ref):
    @pl.when(pl.program_id(1) == 0)
    def _():
        x = x_ref[...]
        inv = lax.rsqrt(jnp.mean(x * x, axis=-1, keepdims=True) + RMS_EPS)
        xn = ((x * inv) * gain_ref[...]).astype(BF16)
        xn_ref[...] = xn
        a = jnp.dot(xn, wa_ref[...], preferred_element_type=F32)
        pre = jnp.dot(a, wgk_ref[...], preferred_element_type=F32, precision=HIGHEST) + bgk_ref[...]
        lg_ref[...] = -_softplus(-pre) * (1.0 / GLA_TAU)

    y = jnp.dot(xn_ref[...], w_ref[...], preferred_element_type=F32) * scale_ref[...]
    p32_ref[...] = y
    p16_ref[...] = y.astype(BF16)


def _inproj(x_all, gain, w_main, colscale, w_a, w_gk, b_gk):
    rows = x_all.shape[0]
    tm = _pick_tile(rows, (640, 512, 256, 128))
    tn = 1024
    return pl.pallas_call(
        _inproj_kernel,
        grid=(rows // tm, D_MAIN // tn),
        in_specs=[
            pl.BlockSpec((tm, D_MODEL), lambda i, j: (i, 0)),
            pl.BlockSpec((1, D_MODEL), lambda i, j: (0, 0)),
            pl.BlockSpec((D_MODEL, tn), lambda i, j: (0, j)),
            pl.BlockSpec((1, tn), lambda i, j: (0, j)),
            pl.BlockSpec((D_MODEL, LANES), lambda i, j: (0, 0)),
            pl.BlockSpec((LANES, GLA_KEY_WIDTH), lambda i, j: (0, 0)),
            pl.BlockSpec((1, GLA_KEY_WIDTH), lambda i, j: (0, 0)),
        ],
        out_specs=[
            pl.BlockSpec((tm, tn), lambda i, j: (i, j)),
            pl.BlockSpec((tm, tn), lambda i, j: (i, j)),
            pl.BlockSpec((tm, GLA_KEY_WIDTH), lambda i, j: (i, 0)),
        ],
        out_shape=[
            jax.ShapeDtypeStruct((rows, D_MAIN), F32),
            jax.ShapeDtypeStruct((rows, D_MAIN), BF16),
            jax.ShapeDtypeStruct((rows, GLA_KEY_WIDTH), F32),
        ],
        scratch_shapes=[pltpu.VMEM((tm, D_MODEL), BF16)],
        compiler_params=_cparams(("arbitrary", "arbitrary")),
    )(x_all, gain, w_main, colscale, w_a, w_gk, b_gk)


def _sb_softplus2(z, allowed):
    neg_abs = pltpu.bitcast(pltpu.bitcast(z, jnp.uint32) | jnp.uint32(0x80000000), F32)
    sp = jnp.maximum(z, 0.0) + jnp.log2(1.0 + jnp.exp2(neg_abs))
    if allowed is not None:
        sp = jnp.where(allowed, sp, 0.0)
    return sp


def _sb_mass(z, u, allowed):
    return jnp.dot(_sb_softplus2(z, allowed).astype(BF16), u, preferred_element_type=F32)


def _sb_weight(z, r, c, allowed):
    w = jnp.exp2(z - r - c)
    if allowed is not None:
        w = jnp.where(allowed, w, 0.0)
    return w.astype(BF16)


def _sb_visit_many(qs, kblks, vblks, u, cs, alloweds):
    tk = u.shape[0]
    n = len(alloweds)
    zs = [lax.dot_general(q, kblk, (((1,), (1,)), ((), ())), preferred_element_type=F32)
          for q, kblk in zip(qs, kblks)]
    rs = [[_sb_mass(z[:, s * tk:(s + 1) * tk], u, alloweds[s]) for s in range(n)] for z in zs]
    ws, c_out = [], []
    for z, r, c in zip(zs, rs, cs):
        parts = [None] * n
        for s in reversed(range(n)):
            parts[s] = _sb_weight(z[:, s * tk:(s + 1) * tk], r[s], c, alloweds[s])
            c = c + r[s][:, 0:1]
        ws.append(parts[0] if n == 1 else jnp.concatenate(parts, axis=1))
        c_out.append(c)
    pvs = [jnp.dot(w, vblk, preferred_element_type=F32) for w, vblk in zip(ws, vblks)]
    return list(zip(pvs, c_out))


def _sb_visit(q, kblk, vblk, u, c, alloweds):
    return _sb_visit_many([q], [kblk], [vblk], u, [c], alloweds)[0]


def _sb_prompt_kernel(q_ref, k_ref, v_ref, km_ref, vm_ref, u_ref, o_ref,
                      acc_ref, c_ref, z_ref, sp_ref, t_ref, tot_ref):
    i = pl.program_id(1)
    q = q_ref[pl.ds(pl.multiple_of(i * SB_TQ, SB_TQ), SB_TQ), :]
    u = u_ref[...]
    n_sub = SB_TQ // SB_TK

    def key_rows(blk):
        return pl.ds(pl.multiple_of(blk * SB_TQ, SB_TQ), SB_TQ)

    def score_matmul(blk):
        return lax.dot_general(q, k_ref[key_rows(blk), :], (((1,), (1,)), ((), ())),
                               preferred_element_type=F32)

    def stage_softplus(z, slot, alloweds):
        z_ref[slot] = z
        for s in range(n_sub):
            cols = slice(s * SB_TK, (s + 1) * SB_TK)
            sp_ref[slot, :, cols] = _sb_softplus2(z[:, cols], alloweds[s]).astype(BF16)

    def stage_mass(slot, alloweds):
        for s in range(n_sub):
            cols = slice(s * SB_TK, (s + 1) * SB_TK)
            r_s = jnp.dot(sp_ref[slot, :, cols], u, preferred_element_type=F32)
            t_s = z_ref[slot, :, cols] - r_s
            if alloweds[s] is not None:
                t_s = jnp.where(alloweds[s], t_s, -jnp.inf)
            t_ref[:, cols] = t_s
            tot_ref[s] = r_s[:, 0:1]

    def stage_accumulate(blk):
        live = blk >= 0
        c = c_ref[...]
        c_eff = c + jnp.where(live, 0.0, jnp.inf)
        ws = [None] * n_sub
        for s in reversed(range(n_sub)):
            ws[s] = jnp.exp2(t_ref[:, s * SB_TK:(s + 1) * SB_TK] - c_eff).astype(BF16)
            c_eff = c_eff + tot_ref[s]
            c = c + jnp.where(live, tot_ref[s], 0.0)
        acc_ref[...] += jnp.dot(jnp.concatenate(ws, axis=1), v_ref[key_rows(jnp.maximum(blk, 0)), :],
                                preferred_element_type=F32)
        c_ref[...] = c

    acc_ref[...] = jnp.zeros_like(acc_ref)
    c_ref[...] = jnp.zeros_like(c_ref)
    row_i = lax.broadcasted_iota(jnp.int32, (SB_TQ, SB_TK), 0)
    col_i = lax.broadcasted_iota(jnp.int32, (SB_TQ, SB_TK), 1)
    own = [col_i + s * SB_TK < row_i for s in range(n_sub)]
    free = [None] * n_sub
    stage_softplus(score_matmul(i), 0, own)
    stage_mass(0, own)
    stage_softplus(score_matmul(jnp.maximum(i - 1, 0)), 1, free)

    def step(p, slot):
        z = score_matmul(jnp.maximum(i - p - 2, 0))
        stage_accumulate(i - p)
        stage_mass(1 - slot, free)
        stage_softplus(z, slot, free)

    def body(n, carry):
        step(2 * n, 0)
        step(2 * n + 1, 1)
        return carry

    lax.fori_loop(0, (i + 2) // 2, body, 0)

    meta_cols = lax.broadcasted_iota(jnp.int32, (SB_TQ, META_BLOCK), 1)
    pv, _ = _sb_visit(q, km_ref[...], vm_ref[...], u[:META_BLOCK, :META_BLOCK], c_ref[...],
                      [meta_cols < N_META])
    o_ref[...] = (acc_ref[...] + pv).astype(o_ref.dtype)


def _sb_prompt(p16, u, n_prompt, n_tok):
    head_spec = lambda off: pl.BlockSpec((n_prompt, SB_DIM), lambda h, i: (0, off + h))
    meta_spec = lambda off: pl.BlockSpec((META_BLOCK, SB_DIM), lambda h, i: (n_tok // META_BLOCK, off + h))
    return pl.pallas_call(
        _sb_prompt_kernel,
        grid=(SB_HEADS, n_prompt // SB_TQ),
        in_specs=[head_spec(0), head_spec(SB_HEADS), head_spec(2 * SB_HEADS),
                  meta_spec(SB_HEADS), meta_spec(2 * SB_HEADS),
                  pl.BlockSpec((SB_TK, SB_TK), lambda h, i: (0, 0))],
        out_specs=pl.BlockSpec((SB_TQ, SB_DIM), lambda h, i: (i, h)),
        out_shape=jax.ShapeDtypeStruct((n_tok, SB_WIDTH), BF16),
        scratch_shapes=[pltpu.VMEM((SB_TQ, SB_DIM), F32), pltpu.VMEM((SB_TQ, 1), F32),
                        pltpu.VMEM((2, SB_TQ, SB_TQ), F32), pltpu.VMEM((2, SB_TQ, SB_TQ), BF16),
                        pltpu.VMEM((SB_TQ, SB_TQ), F32), pltpu.VMEM((SB_TQ // SB_TK, SB_TQ, 1), F32)],
        compiler_params=_cparams(("arbitrary", "arbitrary")),
    )(p16, p16, p16, p16, p16, u)


SB_CACHE_SPLIT = 2


def _sb_sample_kernel(q_ref, kn_ref, vn_ref, kc_ref, vc_ref, u_ref, o_prev_ref, o_ref, acc_ref, c_ref, *,
                      n_new, n_pos):
    first = pl.program_id(1) == 0

    @pl.when(first & (pl.program_id(0) == 0))
    def _():
        acc_ref[...] = jnp.zeros_like(acc_ref)
        c_ref[...] = jnp.zeros_like(c_ref)

    u = u_ref[...]
    r_i = lax.broadcasted_iota(jnp.int32, (n_new, n_new), 0)
    c_i = lax.broadcasted_iota(jnp.int32, (n_new, n_new), 1)
    heads = range(SB_HEADS)
    cols = [slice(head * SB_DIM, (head + 1) * SB_DIM) for head in heads]
    qs = [q_ref[:, cols[head]] for head in heads]
    new = _sb_visit_many(qs, [kn_ref[:, cols[head]] for head in heads],
                         [vn_ref[:, cols[head]] for head in heads], u[:n_new, :n_new],
                         [jnp.zeros((n_new, 1), F32)] * SB_HEADS, [c_i < r_i])
    accs = [jnp.where(first, new[head][0], acc_ref[head]) for head in heads]
    cs = [jnp.where(first, new[head][1], c_ref[head]) for head in heads]
    kblks = [kc_ref[pl.ds(head, n_pos, stride=SB_HEADS), :].astype(BF16) for head in heads]
    vblks = [vc_ref[pl.ds(head, n_pos, stride=SB_HEADS), :].astype(BF16) for head in heads]
    old = _sb_visit_many(qs, kblks, vblks, u, cs, [None] * (n_pos // SB_TK))
    for head in heads:
        acc = accs[head] + old[head][0]
        acc_ref[head] = acc
        c_ref[head] = old[head][1]
        o_ref[:, cols[head]] = acc.astype(o_ref.dtype)


def _sb_sample(p16, cache_k, cache_v, u, o_all, n_prompt, n_batch, n_new):
    n_past = cache_k.shape[1] // SB_HEADS
    n_pos = n_past // SB_CACHE_SPLIT
    assert n_pos % SB_TK == 0
    row0 = n_prompt // n_new
    new_spec = lambda blk: pl.BlockSpec((n_new, SB_WIDTH), lambda b, j: (row0 + b, blk))
    cache_spec = pl.BlockSpec((None, n_pos * SB_HEADS, SB_DIM), lambda b, j: (b, SB_CACHE_SPLIT - 1 - j, 0))
    return pl.pallas_call(
        functools.partial(_sb_sample_kernel, n_new=n_new, n_pos=n_pos),
        grid=(n_batch, SB_CACHE_SPLIT),
        in_specs=[new_spec(0), new_spec(1), new_spec(2), cache_spec, cache_spec,
                  pl.BlockSpec((SB_TK, SB_TK), lambda b, j: (0, 0)),
                  pl.BlockSpec(memory_space=pl.ANY)],
        out_specs=pl.BlockSpec((n_new, SB_WIDTH), lambda b, j: (row0 + b, 0)),
        out_shape=jax.ShapeDtypeStruct(o_all.shape, o_all.dtype),
        input_output_aliases={6: 0},
        scratch_shapes=[pltpu.VMEM((SB_HEADS, n_new, SB_DIM), F32), pltpu.VMEM((SB_HEADS, n_new, 1), F32)],
        compiler_params=_cparams(("arbitrary", "arbitrary")),
    )(p16, p16, p16, cache_k, cache_v, u, o_all)


GLA_LEVELS = 6


def _gla_constants():
    c = GLA_CHUNK
    tri = np.tril(np.ones((c, c), np.float32))
    sel = np.zeros((GLA_LEVELS * c, c), np.float32)
    for lvl in range(GLA_LEVELS):
        n = c >> lvl
        for row in range(c):
            sel[lvl * c + row, (row // n) * n + n // 2 - 1] = 1.0
    return jnp.asarray(tri), jnp.asarray(sel, BF16)


def _select_rows_exact(sel, x):
    width = x.shape[1]
    p1 = x.astype(BF16)
    rem = x - p1.astype(F32)
    p2 = rem.astype(BF16)
    p3 = (rem - p2.astype(F32)).astype(BF16)
    out = jnp.dot(sel, jnp.concatenate([p1, p2, p3], axis=1), preferred_element_type=F32)
    return (out[:, :width] + out[:, width:2 * width]) + out[:, 2 * width:]


def _gla_intra(items, tri, sel):
    c = GLA_CHUNK
    n = len(items)
    t_row = lax.broadcasted_iota(jnp.int32, (c, 1), 0)
    t_idx = lax.broadcasted_iota(jnp.int32, (c, c), 0)
    s_idx = lax.broadcasted_iota(jnp.int32, (c, c), 1)
    qk = lambda a, bb: lax.dot_general(a.astype(BF16), bb.astype(BF16), (((1,), (1,)), ((), ())),
                                       preferred_element_type=F32)
    bs = [jnp.dot(tri, lg, preferred_element_type=F32, precision=HIGHEST) for _, _, _, lg in items]
    refs = [_select_rows_exact(sel, b) for b in bs]
    atts = [jnp.where(t_idx == s_idx, qk(q, k), 0.0) for q, k, _, _ in items]
    for lvl in range(GLA_LEVELS):
        shift = GLA_LEVELS - 1 - lvl
        late = ((t_row >> shift) & 1) == 1
        pair = ((t_idx >> (shift + 1)) == (s_idx >> (shift + 1))) \
            & (((t_idx >> shift) & 1) == 1) & (((s_idx >> shift) & 1) == 0)
        parts = []
        for i, (q, k, _, _) in enumerate(items):
            r = refs[i][lvl * c:(lvl + 1) * c, :]
            qa = jnp.where(late, q * jnp.exp(bs[i] - r), 0.0)
            ka = jnp.where(late, 0.0, k * jnp.exp(r - bs[i]))
            parts.append(qk(qa, ka))
        atts = [atts[i] + jnp.where(pair, parts[i], 0.0) for i in range(n)]
    o_intra = [jnp.dot(atts[i].astype(BF16), items[i][2].astype(BF16), preferred_element_type=F32)
               for i in range(n)]
    out = []
    for i, (q, k, v, _) in enumerate(items):
        b_last = bs[i][c - 1:c, :]
        kd = (k * jnp.exp(b_last - bs[i])).astype(BF16)
        upd = lax.dot_general(v.astype(BF16), kd, (((0,), (0,)), ((), ())), preferred_element_type=F32)
        out.append(((q * jnp.exp(bs[i])).astype(BF16), o_intra[i], jnp.exp(b_last), upd))
    return out


def _gla_kernel(q_ref, k_ref, v_ref, lg_ref, s_in_ref, tri_ref, sel_ref, *rest, n_chunks, chained, n_valid,
                aliased):
    o_ref, s_out_ref, st_ref = rest[1:] if aliased else rest
    step = pl.program_id(0)
    live = None
    if n_valid != GLA_CHUNK:
        live = lax.broadcasted_iota(jnp.int32, (GLA_CHUNK, 1), 0) < n_valid

    if chained:
        @pl.when(step == 0)
        def _():
            for head in range(GLA_HEADS):
                st_ref[head] = s_in_ref[0, head].T

    items = []
    for j in range(n_chunks):
        rows = slice(j * GLA_CHUNK, (j + 1) * GLA_CHUNK)
        for head in range(GLA_HEADS):
            ks = slice(head * GLA_DK, (head + 1) * GLA_DK)
            k = k_ref[rows, ks]
            lg = lg_ref[rows, ks]
            if live is not None:
                k = jnp.where(live, k, 0.0)
                lg = jnp.where(live, lg, 0.0)
            items.append((q_ref[rows, ks], k, v_ref[rows, head * GLA_DV:(head + 1) * GLA_DV], lg))
    intra = _gla_intra(items, tri_ref[...], sel_ref[...])

    states = [st_ref[head] for head in range(GLA_HEADS)] if chained else None
    for j in range(n_chunks):
        rows = slice(j * GLA_CHUNK, (j + 1) * GLA_CHUNK)
        for head in range(GLA_HEADS):
            qe, o_intra, decay, upd = intra[j * GLA_HEADS + head]
            st = states[head] if chained else s_in_ref[j, head].T
            o_ref[rows, head * GLA_DV:(head + 1) * GLA_DV] = o_intra + lax.dot_general(
                qe, st.astype(BF16), (((1,), (1,)), ((), ())), preferred_element_type=F32)
            st = st * decay + upd
            if chained:
                states[head] = st
            else:
                s_out_ref[j, head] = st.T

    if chained:
        for head in range(GLA_HEADS):
            st_ref[head] = states[head]

        @pl.when(step == pl.num_programs(0) - 1)
        def _():
            for head in range(GLA_HEADS):
                s_out_ref[0, head] = states[head].T


def _gla(p32, lg, state_in, tri, sel, o_all, *, first_chunk, n_seq_chunks, chained, n_valid=GLA_CHUNK):
    c = GLA_CHUNK
    per_step = _pick_tile(n_seq_chunks, (4, 2, 1))
    assert first_chunk % per_step == 0
    steps = n_seq_chunks // per_step
    row0 = first_chunk // per_step
    q_blk = 3 * SB_WIDTH // GLA_KEY_WIDTH
    v_blk = (3 * SB_WIDTH + 2 * GLA_KEY_WIDTH) // GLA_WIDTH
    n_state = 1 if chained else n_seq_chunks
    state_spec = pl.BlockSpec((1 if chained else per_step, GLA_HEADS, GLA_DK, GLA_DV),
                              (lambda s: (0, 0, 0, 0)) if chained else (lambda s: (s, 0, 0, 0)))
    rows = per_step * c
    args = [p32, p32, p32, lg, state_in, tri, sel]
    if isinstance(o_all, int):
        out_rows, out0, alias_spec, extra = o_all, 0, [], {}
    else:
        out_rows, out0 = o_all.shape[0], row0
        alias_spec, extra = [pl.BlockSpec(memory_space=pl.ANY)], dict(input_output_aliases={len(args): 0})
        args.append(o_all)
    return pl.pallas_call(
        functools.partial(_gla_kernel, n_chunks=per_step, chained=chained, n_valid=n_valid,
                          aliased=bool(alias_spec)),
        grid=(steps,),
        in_specs=[
            pl.BlockSpec((rows, GLA_KEY_WIDTH), lambda s: (row0 + s, q_blk)),
            pl.BlockSpec((rows, GLA_KEY_WIDTH), lambda s: (row0 + s, q_blk + 1)),
            pl.BlockSpec((rows, GLA_WIDTH), lambda s: (row0 + s, v_blk)),
            pl.BlockSpec((rows, GLA_KEY_WIDTH), lambda s: (row0 + s, 0)),
            state_spec,
            pl.BlockSpec(tri.shape, lambda s: (0, 0)),
            pl.BlockSpec(sel.shape, lambda s: (0, 0)),
        ] + alias_spec,
        out_specs=[pl.BlockSpec((rows, GLA_WIDTH), lambda s: (out0 + s, 0)), state_spec],
        out_shape=[jax.ShapeDtypeStruct((out_rows, GLA_WIDTH), F32),
                   jax.ShapeDtypeStruct((n_state, GLA_HEADS, GLA_DK, GLA_DV), F32)],
        scratch_shapes=[pltpu.VMEM((GLA_HEADS, GLA_DV, GLA_DK), F32)],
        compiler_params=_cparams(("arbitrary",)),
        **extra,
    )(*args)


def _mix_kernel(osb_ref, og_ref, gg_ref, x_ref, gnorm_ref, wout_ref, nffn_ref, wr_ref, br_ref, ltri_ref,
                h_ref, xp_ref, idx_ref, gate_ref, rank_ref, cnt_ref, carry_ref):
    step = pl.program_id(0)

    @pl.when(step == 0)
    def _():
        carry_ref[...] = jnp.zeros_like(carry_ref)

    parts = []
    for head in range(GLA_HEADS):
        vs = slice(head * GLA_DV, (head + 1) * GLA_DV)
        og = og_ref[:, vs]
        inv = lax.rsqrt(jnp.mean(og * og, axis=-1, keepdims=True) + RMS_EPS)
        g = gg_ref[:, vs]
        parts.append((((og * inv) * gnorm_ref[...]) * (g * jax.nn.sigmoid(g))).astype(BF16))
    mixed = jnp.dot(jnp.concatenate([osb_ref[...]] + parts, axis=1), wout_ref[...],
                    preferred_element_type=F32)
    h = x_ref[...] + mixed
    h_ref[...] = h

    inv = lax.rsqrt(jnp.mean(h * h, axis=-1, keepdims=True) + RMS_EPS)
    xn = (h * inv) * nffn_ref[...]
    half = D_MODEL // 2
    lo_bits = pltpu.bitcast(xn[:, :half].astype(BF16).astype(F32), jnp.uint32)
    hi_bits = pltpu.bitcast(xn[:, half:].astype(BF16).astype(F32), jnp.uint32)
    xp_ref[...] = (hi_bits & jnp.uint32(0xFFFF0000)) | (lo_bits >> 16)

    x_hi = xn.astype(BF16)
    x_lo = (xn - x_hi.astype(F32)).astype(BF16)
    both = jnp.dot(x_hi, wr_ref[...], preferred_element_type=F32)
    logits = ((both[:, :LANES] + both[:, LANES:])
              + jnp.dot(x_lo, wr_ref[:, :LANES], preferred_element_type=F32)) + br_ref[...]
    tm = logits.shape[0]
    lane = lax.broadcasted_iota(jnp.int32, (tm, LANES), 1)
    vals, sels = [], []
    idx_out = jnp.zeros((tm, LANES), jnp.int32)
    for k in range(TOP_K):
        m = jnp.max(logits, axis=-1, keepdims=True)
        idx = jnp.min(jnp.where(logits == m, lane, LANES), axis=-1, keepdims=True)
        sel = lane == idx
        vals.append(m)
        sels.append(sel)
        idx_out = jnp.where(lane == k, idx, idx_out)
        logits = jnp.where(sel, -jnp.inf, logits)
    exps = [jnp.exp(v - vals[0]) for v in vals]
    denom = exps[0] + exps[1] + exps[2] + exps[3]
    gate_out = jnp.zeros((tm, LANES), F32)
    for k in range(TOP_K):
        gate_out = jnp.where(lane == k, exps[k] / denom, gate_out)
    idx_ref[...] = idx_out
    gate_ref[...] = gate_out

    hot = jnp.zeros((tm, LANES), F32)
    for sel in sels:
        hot = jnp.where(sel, 1.0, hot)
    before = jnp.dot(ltri_ref[...], hot.astype(BF16), preferred_element_type=F32) + carry_ref[...]
    rank_out = jnp.zeros((tm, LANES), jnp.int32)
    for k in range(TOP_K):
        rk = jnp.sum(jnp.where(sels[k], before, 0.0), axis=-1, keepdims=True)
        rank_out = jnp.where(lane == k, rk.astype(jnp.int32), rank_out)
    rank_ref[...] = rank_out
    carry_ref[...] = carry_ref[...] + jnp.sum(hot, axis=0, keepdims=True)
    cnt_ref[...] = carry_ref[...]


def _mix(o_sb, o_g, p32, x_all, gnorm, w_out, nffn, w_r, b_r, n_tok):
    tm = _pick_tile(n_tok, (512, 256, 128))
    g_blk = (3 * SB_WIDTH + 2 * GLA_KEY_WIDTH + GLA_WIDTH) // GLA_WIDTH
    ltri = jnp.asarray(np.tril(np.ones((tm, tm), np.float32), -1), BF16)
    const = lambda shape: pl.BlockSpec(shape, lambda i: (0, 0))
    tile = lambda w: pl.BlockSpec((tm, w), lambda i: (i, 0))
    return pl.pallas_call(
        _mix_kernel,
        grid=(n_tok // tm,),
        in_specs=[tile(SB_WIDTH), tile(GLA_WIDTH),
                  pl.BlockSpec((tm, GLA_WIDTH), lambda i: (i, g_blk)),
                  tile(D_MODEL), const((1, GLA_DV)), const((D_MODEL, D_MODEL)), const((1, D_MODEL)),
                  const((D_MODEL, 2 * LANES)), const((1, LANES)), const((tm, tm))],
        out_specs=[tile(D_MODEL), tile(D_MODEL // 2), tile(LANES), tile(LANES), tile(LANES),
                   const((1, LANES))],
        out_shape=[jax.ShapeDtypeStruct((n_tok, D_MODEL), F32),
                   jax.ShapeDtypeStruct((n_tok, D_MODEL // 2), jnp.uint32),
                   jax.ShapeDtypeStruct((n_tok, LANES), jnp.int32),
                   jax.ShapeDtypeStruct((n_tok, LANES), F32),
                   jax.ShapeDtypeStruct((n_tok, LANES), jnp.int32),
                   jax.ShapeDtypeStruct((1, LANES), F32)],
        scratch_shapes=[pltpu.VMEM((1, LANES), F32)],
        compiler_params=_cparams(("arbitrary",)),
    )(o_sb, o_g, p32, x_all, gnorm, w_out, nffn, w_r, b_r, ltri)


DMA_ISSUE_UNROLL = 8


def _dispatch_kernel(dest_ref, x_ref, xs_ref, sem, *, rows):
    def start(r, carry):
        for k in range(TOP_K):
            pltpu.make_async_copy(x_ref.at[pl.ds(r, 1), :],
                                  xs_ref.at[pl.ds(dest_ref[0, 0, r * TOP_K + k], 1), :], sem).start()
        return carry

    lax.fori_loop(0, rows, start, 0, unroll=DMA_ISSUE_UNROLL)
    for k in range(TOP_K):
        pltpu.make_async_copy(x_ref, xs_ref.at[pl.ds(0, rows), :], sem).wait()


def _dispatch(xp, dest, n_slots):
    n_tok, width = xp.shape
    rows = _pick_tile(n_tok, (256, 128))
    dest3 = dest.reshape(n_tok // rows, 1, rows * TOP_K)
    return pl.pallas_call(
        functools.partial(_dispatch_kernel, rows=rows),
        grid=(n_tok // rows,),
        in_specs=[pl.BlockSpec((1, 1, rows * TOP_K), lambda i: (i, 0, 0), memory_space=pltpu.SMEM),
                  pl.BlockSpec((rows, width), lambda i: (i, 0))],
        out_specs=pl.BlockSpec(memory_space=pl.ANY),
        out_shape=jax.ShapeDtypeStruct((n_slots, width), xp.dtype),
        scratch_shapes=[pltpu.SemaphoreType.DMA(())],
        compiler_params=_cparams(("arbitrary",)),
    )(dest3, xp)


def _unpack_bf16_pairs(packed):
    lo = pltpu.bitcast(packed << 16, F32).astype(BF16)
    hi = pltpu.bitcast(packed & jnp.uint32(0xFFFF0000), F32).astype(BF16)
    return jnp.concatenate([lo, hi], axis=1)


def _expert_changed(be_ref, m):
    prev = be_ref[jnp.maximum(m - 1, 0)]
    return (m == 0) | (be_ref[m] != prev)


def _moe_up_kernel(be_ref, nu_ref, xs_ref, wg_ref, wu_ref, bg_ref, bu_ref, h_ref, w16_ref, *, tf):
    m = pl.program_id(1)

    @pl.when(_expert_changed(be_ref, m))
    def _():
        w16_ref[:, :tf] = wg_ref[...].astype(BF16)
        w16_ref[:, tf:] = wu_ref[...].astype(BF16)

    @pl.when(m < nu_ref[0])
    def _():
        x = _unpack_bf16_pairs(xs_ref[...])
        gu = jnp.dot(x, w16_ref[...], preferred_element_type=F32)
        g = jnp.minimum(gu[:, :tf] + bg_ref[...], SWIGLU_LIMIT)
        u = jnp.clip(gu[:, tf:] + bu_ref[...], -SWIGLU_LIMIT, SWIGLU_LIMIT)
        h_ref[...] = ((u + 1.0) * (g * jax.nn.sigmoid(g * SWIGLU_ALPHA))).astype(h_ref.dtype)


def _moe_down_kernel(be_ref, nu_ref, h_ref, wd_ref, bd_ref, y_ref, w16_ref):
    m = pl.program_id(1)

    @pl.when(_expert_changed(be_ref, m))
    def _():
        w16_ref[...] = wd_ref[...].astype(BF16)

    @pl.when(m < nu_ref[0])
    def _():
        y_ref[...] = jnp.dot(h_ref[...], w16_ref[...], preferred_element_type=F32) + bd_ref[...]


def _moe_experts(xs, block_expert, n_used, w_gate_up, b_gate_up, w_down, b_down):
    n_slots = xs.shape[0]
    n_blocks = n_slots // MOE_TM
    tf = 512
    n_f = D_FF // tf
    live = lambda m, nu: jnp.minimum(m, nu[0] - 1)
    h = pl.pallas_call(
        functools.partial(_moe_up_kernel, tf=tf),
        grid_spec=pltpu.PrefetchScalarGridSpec(
            num_scalar_prefetch=2,
            grid=(n_f, n_blocks),
            in_specs=[
                pl.BlockSpec((MOE_TM, D_MODEL // 2), lambda j, m, be, nu: (live(m, nu), 0)),
                pl.BlockSpec((None, D_MODEL, tf), lambda j, m, be, nu: (be[m], 0, j)),
                pl.BlockSpec((None, D_MODEL, tf), lambda j, m, be, nu: (be[m], 0, n_f + j)),
                pl.BlockSpec((None, 1, tf), lambda j, m, be, nu: (be[m], 0, j)),
                pl.BlockSpec((None, 1, tf), lambda j, m, be, nu: (be[m], 0, n_f + j)),
            ],
            out_specs=pl.BlockSpec((MOE_TM, tf), lambda j, m, be, nu: (live(m, nu), j)),
            scratch_shapes=[pltpu.VMEM((D_MODEL, 2 * tf), BF16)],
        ),
        out_shape=jax.ShapeDtypeStruct((n_slots, D_FF), BF16),
        compiler_params=_cparams(("arbitrary", "arbitrary")),
    )(block_expert, n_used, xs, w_gate_up, w_gate_up, b_gate_up, b_gate_up)

    tn = 1024
    return pl.pallas_call(
        _moe_down_kernel,
        grid_spec=pltpu.PrefetchScalarGridSpec(
            num_scalar_prefetch=2,
            grid=(D_MODEL // tn, n_blocks),
            in_specs=[
                pl.BlockSpec((MOE_TM, D_FF), lambda j, m, be, nu: (live(m, nu), 0)),
                pl.BlockSpec((None, D_FF, tn), lambda j, m, be, nu: (be[m], 0, j)),
                pl.BlockSpec((None, 1, tn), lambda j, m, be, nu: (be[m], 0, j)),
            ],
            out_specs=pl.BlockSpec((MOE_TM, tn), lambda j, m, be, nu: (live(m, nu), j)),
            scratch_shapes=[pltpu.VMEM((D_FF, tn), BF16)],
        ),
        out_shape=jax.ShapeDtypeStruct((n_slots, D_MODEL), F32),
        compiler_params=_cparams(("arbitrary", "arbitrary")),
    )(block_expert, n_used, h, w_down, b_down)


def _combine_kernel(dest_ref, y_ref, gate_ref, h_ref, nf_ref, outp_ref, outs_ref, buf_ref, sem, *,
                    rows, prompt_tiles):
    def start(r, carry):
        for k in range(TOP_K):
            pltpu.make_async_copy(y_ref.at[pl.ds(dest_ref[0, 0, r * TOP_K + k], 1), :],
                                  buf_ref.at[k, pl.ds(r, 1), :], sem).start()
        return carry

    lax.fori_loop(0, rows, start, 0, unroll=DMA_ISSUE_UNROLL)
    for k in range(TOP_K):
        pltpu.make_async_copy(y_ref.at[pl.ds(0, rows), :], buf_ref.at[k], sem).wait()

    gate = gate_ref[...]
    moe = buf_ref[0] * gate[:, 0:1]
    for k in range(1, TOP_K):
        moe = moe + buf_ref[k] * gate[:, k:k + 1]
    h = h_ref[...] + moe
    inv = lax.rsqrt(jnp.mean(h * h, axis=-1, keepdims=True) + RMS_EPS)
    y = (h * inv) * nf_ref[...]
    is_prompt = pl.program_id(0) < prompt_tiles

    @pl.when(is_prompt)
    def _():
        outp_ref[...] = y

    @pl.when(jnp.logical_not(is_prompt))
    def _():
        outs_ref[...] = y


def _combine(y_slots, dest, gate, h, norm_final, n_prompt):
    n_tok = h.shape[0]
    rows = _pick_tile(math.gcd(n_prompt, n_tok - n_prompt), (256, 128))
    prompt_tiles = n_prompt // rows
    dest3 = dest.reshape(n_tok // rows, 1, rows * TOP_K)
    return pl.pallas_call(
        functools.partial(_combine_kernel, rows=rows, prompt_tiles=prompt_tiles),
        grid=(n_tok // rows,),
        in_specs=[pl.BlockSpec((1, 1, rows * TOP_K), lambda i: (i, 0, 0), memory_space=pltpu.SMEM),
                  pl.BlockSpec(memory_space=pl.ANY),
                  pl.BlockSpec((rows, LANES), lambda i: (i, 0)),
                  pl.BlockSpec((rows, D_MODEL), lambda i: (i, 0)),
                  pl.BlockSpec((1, D_MODEL), lambda i: (0, 0))],
        out_specs=[pl.BlockSpec((rows, D_MODEL), lambda i: (jnp.minimum(i, prompt_tiles - 1), 0)),
                   pl.BlockSpec((rows, D_MODEL), lambda i: (jnp.maximum(i - prompt_tiles, 0), 0))],
        out_shape=[jax.ShapeDtypeStruct((n_prompt, D_MODEL), F32),
                   jax.ShapeDtypeStruct((n_tok - n_prompt, D_MODEL), F32)],
        scratch_shapes=[pltpu.VMEM((TOP_K, rows, D_MODEL), F32), pltpu.SemaphoreType.DMA(())],
        compiler_params=_cparams(("arbitrary",)),
    )(dest3, y_slots, gate, h, norm_final)


def kernel(x_prompt, x_sample, cache_sb_k, cache_sb_v, state_gla, meta_tokens, norm_mix, w_in, w_gk2, b_gk,
           gla_norm, w_out, norm_ffn, w_router, b_router, w_gate_up, b_gate_up, w_down, b_down, norm_final):
    assert x_prompt.shape[0] == 1 and norm_mix.shape[0] == 1, "one prompt stream, one layer"
    n_prompt = x_prompt.shape[1]
    n_batch, n_new = x_sample.shape[:2]
    n_past = cache_sb_k.shape[2]
    assert n_new == GLA_CHUNK and n_prompt % SB_TQ == 0 and n_past % (SB_TK * SB_CACHE_SPLIT) == 0
    n_tok = n_prompt + n_batch * n_new
    assert n_tok % META_BLOCK == 0

    x_all = jnp.concatenate([
        x_prompt[0], x_sample.reshape(n_batch * n_new, D_MODEL), meta_tokens.astype(F32),
        jnp.zeros((META_BLOCK - N_META, D_MODEL), F32)], axis=0)

    w_in0 = w_in[0]
    w_main = w_in0[:, :D_MAIN].astype(BF16)
    w_a = jnp.pad(w_in0[:, D_MAIN:], ((0, 0), (0, LANES - GLA_RANK))).astype(BF16)
    w_gk = jnp.pad(w_gk2[0], ((0, LANES - GLA_RANK), (0, 0)))
    colscale = np.ones((1, D_MAIN), np.float32)
    colscale[0, :SB_WIDTH] = SB_DIM ** -0.5 * LOG2E
    colscale[0, 3 * SB_WIDTH:3 * SB_WIDTH + GLA_KEY_WIDTH] = GLA_DK ** -0.5
    p32, p16, lg = _inproj(x_all, norm_mix, w_main, jnp.asarray(colscale), w_a, w_gk, b_gk)

    u = jnp.asarray(np.tril(np.ones((SB_TK, SB_TK), np.float32)), BF16)
    o_sb = _sb_prompt(p16, u, n_prompt, n_tok)
    o_sb = _sb_sample(p16, cache_sb_k[0].reshape(n_batch, n_past * SB_HEADS, SB_DIM),
                      cache_sb_v[0].reshape(n_batch, n_past * SB_HEADS, SB_DIM), u, o_sb, n_prompt, n_batch, n_new)

    tri, sel = _gla_constants()
    c = GLA_CHUNK
    zero_state = jnp.zeros((1, GLA_HEADS, GLA_DK, GLA_DV), F32)
    _, state_meta = _gla(p32, lg, zero_state, tri, sel, c, first_chunk=n_tok // c, n_seq_chunks=1,
                         chained=True, n_valid=N_META)
    o_g, state_p = _gla(p32, lg, state_meta, tri, sel, n_tok, first_chunk=0, n_seq_chunks=n_prompt // c,
                        chained=True)
    o_g, state_s = _gla(p32, lg, state_gla[0].astype(F32), tri, sel, o_g, first_chunk=n_prompt // c,
                        n_seq_chunks=n_batch, chained=False)

    w_r = jnp.pad(w_router[0].astype(F32), ((0, 0), (0, LANES - N_EXPERTS)))
    w_r_hi = w_r.astype(BF16)
    w_r = jnp.concatenate([w_r_hi, (w_r - w_r_hi.astype(F32)).astype(BF16)], axis=1)
    b_r = jnp.pad(b_router.astype(F32), ((0, 0), (0, LANES - N_EXPERTS)), constant_values=-jnp.inf)
    h, xp, top_idx, gate, rank, counts = _mix(o_sb, o_g, p32, x_all, gla_norm, w_out[0].astype(BF16),
                                              norm_ffn, w_r, b_r, n_tok)

    counts = counts[0, :N_EXPERTS].astype(jnp.int32)
    padded = (counts + MOE_TM - 1) // MOE_TM * MOE_TM
    ends = jnp.cumsum(padded)
    n_blocks = -(-n_tok * TOP_K // MOE_TM) + N_EXPERTS
    dest = ((ends - padded)[top_idx[:, :TOP_K]] + rank[:, :TOP_K]).astype(jnp.int32)
    block_start = jnp.arange(n_blocks, dtype=jnp.int32) * MOE_TM
    block_expert = jnp.minimum(jnp.sum((ends[None, :] <= block_start[:, None]).astype(jnp.int32), axis=1),
                               N_EXPERTS - 1)
    n_used = (ends[-1:] // MOE_TM).astype(jnp.int32)

    xs = _dispatch(xp, dest, n_blocks * MOE_TM)
    y_slots = _moe_experts(xs, block_expert, n_used, w_gate_up[0], b_gate_up[0][:, None, :],
                           w_down[0], b_down[0][:, None, :])
    y_prompt, y_sample = _combine(y_slots, dest, gate, h, norm_final[None, :], n_prompt)
    y_prompt = y_prompt[None]
    y_sample = y_sample.reshape(n_batch, n_new, D_MODEL)
    k_lo, v_lo = SB_WIDTH, 2 * SB_WIDTH

    def kv_prompt(lo):
        rows = jnp.concatenate([p32[n_tok:n_tok + N_META, lo:lo + SB_WIDTH], p32[:n_prompt, lo:lo + SB_WIDTH]])
        return rows.reshape(1, 1, N_META + n_prompt, SB_HEADS, SB_DIM)

    def kv_sample(lo):
        return p32[n_prompt:n_tok, lo:lo + SB_WIDTH].reshape(1, n_batch, n_new, SB_HEADS, SB_DIM)

    return (y_prompt, y_sample, kv_prompt(k_lo), kv_prompt(v_lo), state_p[None],
            kv_sample(k_lo), kv_sample(v_lo), state_s[None])
```

```python
import functools
import math

import jax
import jax.numpy as jnp
import numpy as np
from jax import lax
from jax.experimental import pallas as pl
from jax.experimental.pallas import tpu as pltpu

F32 = jnp.float32
BF16 = jnp.bfloat16
HIGHEST = lax.Precision.HIGHEST

D_MODEL = 2048
N_META = 16
META_BLOCK = 128
SB_HEADS = 8
SB_DIM = 128
SB_WIDTH = SB_HEADS * SB_DIM
GLA_HEADS = 4
GLA_DK = 128
GLA_DV = 256
GLA_KEY_WIDTH = GLA_HEADS * GLA_DK
GLA_WIDTH = GLA_HEADS * GLA_DV
GLA_RANK = 16
GLA_TAU = 16.0
GLA_CHUNK = 64
N_EXPERTS = 32
TOP_K = 4
D_FF = 2048
SWIGLU_LIMIT = 7.0
SWIGLU_ALPHA = 1.702
RMS_EPS = 1e-5
D_MAIN = 3 * SB_WIDTH + 2 * GLA_KEY_WIDTH + 2 * GLA_WIDTH
LANES = 128
LOG2E = 1.4426950408889634

SB_TQ = 512
SB_TK = 256
SB_STEPS_PER_TRIP = 4
MOE_TM = 512
VMEM_LIMIT = 56 * 1024 * 1024


def _cparams(sem):
    return pltpu.CompilerParams(dimension_semantics=sem, vmem_limit_bytes=VMEM_LIMIT)


def _pick_tile(n, cands):
    for c in cands:
        if n % c == 0:
            return c
    raise ValueError(f"no tile for {n} in {cands}")


def _softplus(z):
    return jnp.maximum(z, 0.0) + jnp.log(1.0 + jnp.exp(-jnp.abs(z)))


def _inproj_kernel(x_ref, gain_ref, w_ref, scale_ref, wa_ref, wgk_ref, bgk_ref,
                   p32_ref, p16_ref, lg_ref, xn_ref):
    @pl.when(pl.program_id(1) == 0)
    def _():
        x = x_ref[...]
        inv = lax.rsqrt(jnp.mean(x * x, axis=-1, keepdims=True) + RMS_EPS)
        xn = ((x * inv) * gain_ref[...]).astype(BF16)
        xn_ref[...] = xn
        a = jnp.dot(xn, wa_ref[...], preferred_element_type=F32)
        pre = jnp.dot(a, wgk_ref[...], preferred_element_type=F32, precision=HIGHEST) + bgk_ref[...]
        lg_ref[...] = -_softplus(-pre) * (1.0 / GLA_TAU)

    y = jnp.dot(xn_ref[...], w_ref[...], preferred_element_type=F32) * scale_ref[...]
    p32_ref[...] = y
    p16_ref[...] = y.astype(BF16)


def _inproj(x_all, gain, w_main, colscale, w_a, w_gk, b_gk):
    rows = x_all.shape[0]
    tm = _pick_tile(rows, (640, 512, 256, 128))
    tn = 1024
    return pl.pallas_call(
        _inproj_kernel,
        grid=(rows // tm, D_MAIN // tn),
        in_specs=[
            pl.BlockSpec((tm, D_MODEL), lambda i, j: (i, 0)),
            pl.BlockSpec((1, D_MODEL), lambda i, j: (0, 0)),
            pl.BlockSpec((D_MODEL, tn), lambda i, j: (0, j)),
            pl.BlockSpec((1, tn), lambda i, j: (0, j)),
            pl.BlockSpec((D_MODEL, LANES), lambda i, j: (0, 0)),
            pl.BlockSpec((LANES, GLA_KEY_WIDTH), lambda i, j: (0, 0)),
            pl.BlockSpec((1, GLA_KEY_WIDTH), lambda i, j: (0, 0)),
        ],
        out_specs=[
            pl.BlockSpec((tm, tn), lambda i, j: (i, j)),
            pl.BlockSpec((tm, tn), lambda i, j: (i, j)),
            pl.BlockSpec((tm, GLA_KEY_WIDTH), lambda i, j: (i, 0)),
        ],
        out_shape=[
            jax.ShapeDtypeStruct((rows, D_MAIN), F32),
            jax.ShapeDtypeStruct((rows, D_MAIN), BF16),
            jax.ShapeDtypeStruct((rows, GLA_KEY_WIDTH), F32),
        ],
        scratch_shapes=[pltpu.VMEM((tm, D_MODEL), BF16)],
        compiler_params=_cparams(("arbitrary", "arbitrary")),
    )(x_all, gain, w_main, colscale, w_a, w_gk, b_gk)


def _sb_softplus2(z, allowed):
    neg_abs = pltpu.bitcast(pltpu.bitcast(z, jnp.uint32) | jnp.uint32(0x80000000), F32)
    sp = jnp.maximum(z, 0.0) + jnp.log2(1.0 + jnp.exp2(neg_abs))
    if allowed is not None:
        sp = jnp.where(allowed, sp, 0.0)
    return sp


def _sb_mass(z, u, allowed):
    return jnp.dot(_sb_softplus2(z, allowed).astype(BF16), u, preferred_element_type=F32)


def _sb_weight(z, r, c, allowed):
    w = jnp.exp2(z - r - c)
    if allowed is not None:
        w = jnp.where(allowed, w, 0.0)
    return w.astype(BF16)


def _sb_visit_many(qs, kblks, vblks, u, cs, alloweds):
    tk = u.shape[0]
    n = len(alloweds)
    zs = [lax.dot_general(q, kblk, (((1,), (1,)), ((), ())), preferred_element_type=F32)
          for q, kblk in zip(qs, kblks)]
    rs = [[_sb_mass(z[:, s * tk:(s + 1) * tk], u, alloweds[s]) for s in range(n)] for z in zs]
    ws, c_out = [], []
    for z, r, c in zip(zs, rs, cs):
        parts = [None] * n
        for s in reversed(range(n)):
            parts[s] = _sb_weight(z[:, s * tk:(s + 1) * tk], r[s], c, alloweds[s])
            c = c + r[s][:, 0:1]
        ws.append(parts[0] if n == 1 else jnp.concatenate(parts, axis=1))
        c_out.append(c)
    pvs = [jnp.dot(w, vblk, preferred_element_type=F32) for w, vblk in zip(ws, vblks)]
    return list(zip(pvs, c_out))


def _sb_visit(q, kblk, vblk, u, c, alloweds):
    return _sb_visit_many([q], [kblk], [vblk], u, [c], alloweds)[0]


def _sb_prompt_kernel(q_ref, k_ref, v_ref, km_ref, vm_ref, u_ref, o_ref,
                      acc_ref, c_ref, z_ref, sp_ref, t_ref, tot_ref):
    i = pl.program_id(1)
    q = q_ref[pl.ds(pl.multiple_of(i * SB_TQ, SB_TQ), SB_TQ), :]
    u = u_ref[...]
    n_sub = SB_TQ // SB_TK

    def key_rows(blk):
        return pl.ds(pl.multiple_of(blk * SB_TQ, SB_TQ), SB_TQ)

    def score_matmul(blk):
        return lax.dot_general(q, k_ref[key_rows(blk), :], (((1,), (1,)), ((), ())),
                               preferred_element_type=F32)

    def stage_softplus(z, slot, alloweds):
        z_ref[slot] = z
        for s in range(n_sub):
            cols = slice(s * SB_TK, (s + 1) * SB_TK)
            sp_ref[slot, :, cols] = _sb_softplus2(z[:, cols], alloweds[s]).astype(BF16)

    def stage_mass(slot, alloweds):
        for s in range(n_sub):
            cols = slice(s * SB_TK, (s + 1) * SB_TK)
            r_s = jnp.dot(sp_ref[slot, :, cols], u, preferred_element_type=F32)
            t_s = z_ref[slot, :, cols] - r_s
            if alloweds[s] is not None:
                t_s = jnp.where(alloweds[s], t_s, -jnp.inf)
            t_ref[:, cols] = t_s
            tot_ref[s] = r_s[:, 0:1]

    def stage_accumulate(blk):
        c = c_ref[...]
        ws = [None] * n_sub
        for s in reversed(range(n_sub)):
            ws[s] = jnp.exp2(t_ref[:, s * SB_TK:(s + 1) * SB_TK] - c).astype(BF16)
            c = c + tot_ref[s]
        acc_ref[...] += jnp.dot(jnp.concatenate(ws, axis=1), v_ref[key_rows(blk), :],
                                preferred_element_type=F32)
        c_ref[...] = c

    acc_ref[...] = jnp.zeros_like(acc_ref)
    c_ref[...] = jnp.zeros_like(c_ref)
    row_i = lax.broadcasted_iota(jnp.int32, (SB_TQ, SB_TK), 0)
    col_i = lax.broadcasted_iota(jnp.int32, (SB_TQ, SB_TK), 1)
    own = [col_i + s * SB_TK < row_i for s in range(n_sub)]
    free = [None] * n_sub
    stage_softplus(score_matmul(i), 0, own)
    stage_mass(0, own)
    stage_softplus(score_matmul(jnp.maximum(i - 1, 0)), 1, free)

    def step(p, slot):
        stage_softplus(score_matmul(i - p - 2), slot, free)
        stage_accumulate(i - p)
        stage_mass(1 - slot, free)

    def body(n, carry):
        for k in range(SB_STEPS_PER_TRIP):
            step(SB_STEPS_PER_TRIP * n + k, k % 2)
        return carry

    n_full = jnp.maximum(i - 1, 0)
    lax.fori_loop(0, n_full // SB_STEPS_PER_TRIP, body, 0)
    done = n_full - n_full % SB_STEPS_PER_TRIP
    for k in range(SB_STEPS_PER_TRIP - 1):
        @pl.when(n_full - done > k)
        def _():
            step(done + k, k % 2)

    @pl.when(i >= 1)
    def _():
        stage_accumulate(1)
        stage_mass(i % 2, free)

    stage_accumulate(0)

    meta_cols = lax.broadcasted_iota(jnp.int32, (SB_TQ, META_BLOCK), 1)
    pv, _ = _sb_visit(q, km_ref[...], vm_ref[...], u[:META_BLOCK, :META_BLOCK], c_ref[...],
                      [meta_cols < N_META])
    o_ref[...] = (acc_ref[...] + pv).astype(o_ref.dtype)


def _sb_prompt(p16, u, n_prompt, n_tok):
    head_spec = lambda off: pl.BlockSpec((n_prompt, SB_DIM), lambda h, i: (0, off + h))
    meta_spec = lambda off: pl.BlockSpec((META_BLOCK, SB_DIM), lambda h, i: (n_tok // META_BLOCK, off + h))
    return pl.pallas_call(
        _sb_prompt_kernel,
        grid=(SB_HEADS, n_prompt // SB_TQ),
        in_specs=[head_spec(0), head_spec(SB_HEADS), head_spec(2 * SB_HEADS),
                  meta_spec(SB_HEADS), meta_spec(2 * SB_HEADS),
                  pl.BlockSpec((SB_TK, SB_TK), lambda h, i: (0, 0))],
        out_specs=pl.BlockSpec((SB_TQ, SB_DIM), lambda h, i: (i, h)),
        out_shape=jax.ShapeDtypeStruct((n_tok, SB_WIDTH), BF16),
        scratch_shapes=[pltpu.VMEM((SB_TQ, SB_DIM), F32), pltpu.VMEM((SB_TQ, 1), F32),
                        pltpu.VMEM((2, SB_TQ, SB_TQ), F32), pltpu.VMEM((2, SB_TQ, SB_TQ), BF16),
                        pltpu.VMEM((SB_TQ, SB_TQ), F32), pltpu.VMEM((SB_TQ // SB_TK, SB_TQ, 1), F32)],
        compiler_params=_cparams(("arbitrary", "arbitrary")),
    )(p16, p16, p16, p16, p16, u)


SB_CACHE_SPLIT = 2


def _sb_sample_kernel(q_ref, kn_ref, vn_ref, kc_ref, vc_ref, u_ref, o_prev_ref, o_ref, acc_ref, c_ref, *,
                      n_new, n_pos):
    first = pl.program_id(1) == 0

    @pl.when(first & (pl.program_id(0) == 0))
    def _():
        acc_ref[...] = jnp.zeros_like(acc_ref)
        c_ref[...] = jnp.zeros_like(c_ref)

    u = u_ref[...]
    r_i = lax.broadcasted_iota(jnp.int32, (n_new, n_new), 0)
    c_i = lax.broadcasted_iota(jnp.int32, (n_new, n_new), 1)
    heads = range(SB_HEADS)
    cols = [slice(head * SB_DIM, (head + 1) * SB_DIM) for head in heads]
    qs = [q_ref[:, cols[head]] for head in heads]
    new = _sb_visit_many(qs, [kn_ref[:, cols[head]] for head in heads],
                         [vn_ref[:, cols[head]] for head in heads], u[:n_new, :n_new],
                         [jnp.zeros((n_new, 1), F32)] * SB_HEADS, [c_i < r_i])
    accs = [jnp.where(first, new[head][0], acc_ref[head]) for head in heads]
    cs = [jnp.where(first, new[head][1], c_ref[head]) for head in heads]
    kblks = [kc_ref[pl.ds(head, n_pos, stride=SB_HEADS), :].astype(BF16) for head in heads]
    vblks = [vc_ref[pl.ds(head, n_pos, stride=SB_HEADS), :].astype(BF16) for head in heads]
    old = _sb_visit_many(qs, kblks, vblks, u, cs, [None] * (n_pos // SB_TK))
    for head in heads:
        acc = accs[head] + old[head][0]
        acc_ref[head] = acc
        c_ref[head] = old[head][1]
        o_ref[:, cols[head]] = acc.astype(o_ref.dtype)


def _sb_sample(p16, cache_k, cache_v, u, o_all, n_prompt, n_batch, n_new):
    n_past = cache_k.shape[1] // SB_HEADS
    n_pos = n_past // SB_CACHE_SPLIT
    assert n_pos % SB_TK == 0
    row0 = n_prompt // n_new
    new_spec = lambda blk: pl.BlockSpec((n_new, SB_WIDTH), lambda b, j: (row0 + b, blk))
    cache_spec = pl.BlockSpec((None, n_pos * SB_HEADS, SB_DIM), lambda b, j: (b, SB_CACHE_SPLIT - 1 - j, 0))
    return pl.pallas_call(
        functools.partial(_sb_sample_kernel, n_new=n_new, n_pos=n_pos),
        grid=(n_batch, SB_CACHE_SPLIT),
        in_specs=[new_spec(0), new_spec(1), new_spec(2), cache_spec, cache_spec,
                  pl.BlockSpec((SB_TK, SB_TK), lambda b, j: (0, 0)),
                  pl.BlockSpec(memory_space=pl.ANY)],
        out_specs=pl.BlockSpec((n_new, SB_WIDTH), lambda b, j: (row0 + b, 0)),
        out_shape=jax.ShapeDtypeStruct(o_all.shape, o_all.dtype),
        input_output_aliases={6: 0},
        scratch_shapes=[pltpu.VMEM((SB_HEADS, n_new, SB_DIM), F32), pltpu.VMEM((SB_HEADS, n_new, 1), F32)],
        compiler_params=_cparams(("arbitrary", "arbitrary")),
    )(p16, p16, p16, cache_k, cache_v, u, o_all)


GLA_LEVELS = 6


def _gla_constants():
    c = GLA_CHUNK
    tri = np.tril(np.ones((c, c), np.float32))
    sel = np.zeros((GLA_LEVELS * c, c), np.float32)
    for lvl in range(GLA_LEVELS):
        n = c >> lvl
        for row in range(c):
            sel[lvl * c + row, (row // n) * n + n // 2 - 1] = 1.0
    return jnp.asarray(tri), jnp.asarray(sel, BF16)


def _select_rows_exact(sel, x):
    width = x.shape[1]
    p1 = x.astype(BF16)
    rem = x - p1.astype(F32)
    p2 = rem.astype(BF16)
    p3 = (rem - p2.astype(F32)).astype(BF16)
    out = jnp.dot(sel, jnp.concatenate([p1, p2, p3], axis=1), preferred_element_type=F32)
    return (out[:, :width] + out[:, width:2 * width]) + out[:, 2 * width:]


def _gla_intra(items, tri, sel):
    c = GLA_CHUNK
    n = len(items)
    t_row = lax.broadcasted_iota(jnp.int32, (c, 1), 0)
    t_idx = lax.broadcasted_iota(jnp.int32, (c, c), 0)
    s_idx = lax.broadcasted_iota(jnp.int32, (c, c), 1)
    qk = lambda a, bb: lax.dot_general(a.astype(BF16), bb.astype(BF16), (((1,), (1,)), ((), ())),
                                       preferred_element_type=F32)
    bs = [jnp.dot(tri, lg, preferred_element_type=F32, precision=HIGHEST) for _, _, _, lg in items]
    refs = [_select_rows_exact(sel, b) for b in bs]
    atts = [jnp.where(t_idx == s_idx, qk(q, k), 0.0) for q, k, _, _ in items]
    for lvl in range(GLA_LEVELS):
        shift = GLA_LEVELS - 1 - lvl
        late = ((t_row >> shift) & 1) == 1
        pair = ((t_idx >> (shift + 1)) == (s_idx >> (shift + 1))) \
            & (((t_idx >> shift) & 1) == 1) & (((s_idx >> shift) & 1) == 0)
        parts = []
        for i, (q, k, _, _) in enumerate(items):
            r = refs[i][lvl * c:(lvl + 1) * c, :]
            qa = jnp.where(late, q * jnp.exp(bs[i] - r), 0.0)
            ka = jnp.where(late, 0.0, k * jnp.exp(r - bs[i]))
            parts.append(qk(qa, ka))
        atts = [atts[i] + jnp.where(pair, parts[i], 0.0) for i in range(n)]
    o_intra = [jnp.dot(atts[i].astype(BF16), items[i][2].astype(BF16), preferred_element_type=F32)
               for i in range(n)]
    out = []
    for i, (q, k, v, _) in enumerate(items):
        b_last = bs[i][c - 1:c, :]
        kd = (k * jnp.exp(b_last - bs[i])).astype(BF16)
        upd = lax.dot_general(v.astype(BF16), kd, (((0,), (0,)), ((), ())), preferred_element_type=F32)
        out.append(((q * jnp.exp(bs[i])).astype(BF16), o_intra[i], jnp.exp(b_last), upd))
    return out


def _gla_kernel(q_ref, k_ref, v_ref, lg_ref, s_in_ref, tri_ref, sel_ref, *rest, n_chunks, chained, n_valid,
                aliased):
    o_ref, s_out_ref, st_ref = rest[1:] if aliased else rest
    step = pl.program_id(0)
    live = None
    if n_valid != GLA_CHUNK:
        live = lax.broadcasted_iota(jnp.int32, (GLA_CHUNK, 1), 0) < n_valid

    if chained:
        @pl.when(step == 0)
        def _():
            for head in range(GLA_HEADS):
                st_ref[head] = s_in_ref[0, head].T

    items = []
    for j in range(n_chunks):
        rows = slice(j * GLA_CHUNK, (j + 1) * GLA_CHUNK)
        for head in range(GLA_HEADS):
            ks = slice(head * GLA_DK, (head + 1) * GLA_DK)
            k = k_ref[rows, ks]
            lg = lg_ref[rows, ks]
            if live is not None:
                k = jnp.where(live, k, 0.0)
                lg = jnp.where(live, lg, 0.0)
            items.append((q_ref[rows, ks], k, v_ref[rows, head * GLA_DV:(head + 1) * GLA_DV], lg))
    intra = _gla_intra(items, tri_ref[...], sel_ref[...])

    states = [st_ref[head] for head in range(GLA_HEADS)] if chained else None
    for j in range(n_chunks):
        rows = slice(j * GLA_CHUNK, (j + 1) * GLA_CHUNK)
        for head in range(GLA_HEADS):
            qe, o_intra, decay, upd = intra[j * GLA_HEADS + head]
            st = states[head] if chained else s_in_ref[j, head].T
            o_ref[rows, head * GLA_DV:(head + 1) * GLA_DV] = o_intra + lax.dot_general(
                qe, st.astype(BF16), (((1,), (1,)), ((), ())), preferred_element_type=F32)
            st = st * decay + upd
            if chained:
                states[head] = st
            else:
                s_out_ref[j, head] = st.T

    if chained:
        for head in range(GLA_HEADS):
            st_ref[head] = states[head]

        @pl.when(step == pl.num_programs(0) - 1)
        def _():
            for head in range(GLA_HEADS):
                s_out_ref[0, head] = states[head].T


def _gla(p32, lg, state_in, tri, sel, o_all, *, first_chunk, n_seq_chunks, chained, n_valid=GLA_CHUNK):
    c = GLA_CHUNK
    per_step = _pick_tile(n_seq_chunks, (4, 2, 1))
    assert first_chunk % per_step == 0
    steps = n_seq_chunks // per_step
    row0 = first_chunk // per_step
    q_blk = 3 * SB_WIDTH // GLA_KEY_WIDTH
    v_blk = (3 * SB_WIDTH + 2 * GLA_KEY_WIDTH) // GLA_WIDTH
    n_state = 1 if chained else n_seq_chunks
    state_spec = pl.BlockSpec((1 if chained else per_step, GLA_HEADS, GLA_DK, GLA_DV),
                              (lambda s: (0, 0, 0, 0)) if chained else (lambda s: (s, 0, 0, 0)))
    rows = per_step * c
    args = [p32, p32, p32, lg, state_in, tri, sel]
    if isinstance(o_all, int):
        out_rows, out0, alias_spec, extra = o_all, 0, [], {}
    else:
        out_rows, out0 = o_all.shape[0], row0
        alias_spec, extra = [pl.BlockSpec(memory_space=pl.ANY)], dict(input_output_aliases={len(args): 0})
        args.append(o_all)
    return pl.pallas_call(
        functools.partial(_gla_kernel, n_chunks=per_step, chained=chained, n_valid=n_valid,
                          aliased=bool(alias_spec)),
        grid=(steps,),
        in_specs=[
            pl.BlockSpec((rows, GLA_KEY_WIDTH), lambda s: (row0 + s, q_blk)),
            pl.BlockSpec((rows, GLA_KEY_WIDTH), lambda s: (row0 + s, q_blk + 1)),
            pl.BlockSpec((rows, GLA_WIDTH), lambda s: (row0 + s, v_blk)),
            pl.BlockSpec((rows, GLA_KEY_WIDTH), lambda s: (row0 + s, 0)),
            state_spec,
            pl.BlockSpec(tri.shape, lambda s: (0, 0)),
            pl.BlockSpec(sel.shape, lambda s: (0, 0)),
        ] + alias_spec,
        out_specs=[pl.BlockSpec((rows, GLA_WIDTH), lambda s: (out0 + s, 0)), state_spec],
        out_shape=[jax.ShapeDtypeStruct((out_rows, GLA_WIDTH), F32),
                   jax.ShapeDtypeStruct((n_state, GLA_HEADS, GLA_DK, GLA_DV), F32)],
        scratch_shapes=[pltpu.VMEM((GLA_HEADS, GLA_DV, GLA_DK), F32)],
        compiler_params=_cparams(("arbitrary",)),
        **extra,
    )(*args)


def _mix_kernel(osb_ref, og_ref, gg_ref, x_ref, gnorm_ref, wout_ref, nffn_ref, wr_ref, br_ref, ltri_ref,
                h_ref, xp_ref, idx_ref, gate_ref, rank_ref, cnt_ref, carry_ref):
    step = pl.program_id(0)

    @pl.when(step == 0)
    def _():
        carry_ref[...] = jnp.zeros_like(carry_ref)

    parts = []
    for head in range(GLA_HEADS):
        vs = slice(head * GLA_DV, (head + 1) * GLA_DV)
        og = og_ref[:, vs]
        inv = lax.rsqrt(jnp.mean(og * og, axis=-1, keepdims=True) + RMS_EPS)
        g = gg_ref[:, vs]
        parts.append((((og * inv) * gnorm_ref[...]) * (g * jax.nn.sigmoid(g))).astype(BF16))
    mixed = jnp.dot(jnp.concatenate([osb_ref[...]] + parts, axis=1), wout_ref[...],
                    preferred_element_type=F32)
    h = x_ref[...] + mixed
    h_ref[...] = h

    inv = lax.rsqrt(jnp.mean(h * h, axis=-1, keepdims=True) + RMS_EPS)
    xn = (h * inv) * nffn_ref[...]
    half = D_MODEL // 2
    lo_bits = pltpu.bitcast(xn[:, :half].astype(BF16).astype(F32), jnp.uint32)
    hi_bits = pltpu.bitcast(xn[:, half:].astype(BF16).astype(F32), jnp.uint32)
    xp_ref[...] = (hi_bits & jnp.uint32(0xFFFF0000)) | (lo_bits >> 16)

    x_hi = xn.astype(BF16)
    x_lo = (xn - x_hi.astype(F32)).astype(BF16)
    both = jnp.dot(x_hi, wr_ref[...], preferred_element_type=F32)
    logits = ((both[:, :LANES] + both[:, LANES:])
              + jnp.dot(x_lo, wr_ref[:, :LANES], preferred_element_type=F32)) + br_ref[...]
    tm = logits.shape[0]
    lane = lax.broadcasted_iota(jnp.int32, (tm, LANES), 1)
    vals, sels = [], []
    idx_out = jnp.zeros((tm, LANES), jnp.int32)
    for k in range(TOP_K):
        m = jnp.max(logits, axis=-1, keepdims=True)
        idx = jnp.min(jnp.where(logits == m, lane, LANES), axis=-1, keepdims=True)
        sel = lane == idx
        vals.append(m)
        sels.append(sel)
        idx_out = jnp.where(lane == k, idx, idx_out)
        logits = jnp.where(sel, -jnp.inf, logits)
    exps = [jnp.exp(v - vals[0]) for v in vals]
    denom = exps[0] + exps[1] + exps[2] + exps[3]
    gate_out = jnp.zeros((tm, LANES), F32)
    for k in range(TOP_K):
        gate_out = jnp.where(lane == k, exps[k] / denom, gate_out)
    idx_ref[...] = idx_out
    gate_ref[...] = gate_out

    hot = jnp.zeros((tm, LANES), F32)
    for sel in sels:
        hot = jnp.where(sel, 1.0, hot)
    before = jnp.dot(ltri_ref[...], hot.astype(BF16), preferred_element_type=F32) + carry_ref[...]
    rank_out = jnp.zeros((tm, LANES), jnp.int32)
    for k in range(TOP_K):
        rk = jnp.sum(jnp.where(sels[k], before, 0.0), axis=-1, keepdims=True)
        rank_out = jnp.where(lane == k, rk.astype(jnp.int32), rank_out)
    rank_ref[...] = rank_out
    carry_ref[...] = carry_ref[...] + jnp.sum(hot, axis=0, keepdims=True)
    cnt_ref[...] = carry_ref[...]


def _mix(o_sb, o_g, p32, x_all, gnorm, w_out, nffn, w_r, b_r, n_tok):
    tm = _pick_tile(n_tok, (512, 256, 128))
    g_blk = (3 * SB_WIDTH + 2 * GLA_KEY_WIDTH + GLA_WIDTH) // GLA_WIDTH
    ltri = jnp.asarray(np.tril(np.ones((tm, tm), np.float32), -1), BF16)
    const = lambda shape: pl.BlockSpec(shape, lambda i: (0, 0))
    tile = lambda w: pl.BlockSpec((tm, w), lambda i: (i, 0))
    return pl.pallas_call(
        _mix_kernel,
        grid=(n_tok // tm,),
        in_specs=[tile(SB_WIDTH), tile(GLA_WIDTH),
                  pl.BlockSpec((tm, GLA_WIDTH), lambda i: (i, g_blk)),
                  tile(D_MODEL), const((1, GLA_DV)), const((D_MODEL, D_MODEL)), const((1, D_MODEL)),
                  const((D_MODEL, 2 * LANES)), const((1, LANES)), const((tm, tm))],
        out_specs=[tile(D_MODEL), tile(D_MODEL // 2), tile(LANES), tile(LANES), tile(LANES),
                   const((1, LANES))],
        out_shape=[jax.ShapeDtypeStruct((n_tok, D_MODEL), F32),
                   jax.ShapeDtypeStruct((n_tok, D_MODEL // 2), jnp.uint32),
                   jax.ShapeDtypeStruct((n_tok, LANES), jnp.int32),
                   jax.ShapeDtypeStruct((n_tok, LANES), F32),
                   jax.ShapeDtypeStruct((n_tok, LANES), jnp.int32),
                   jax.ShapeDtypeStruct((1, LANES), F32)],
        scratch_shapes=[pltpu.VMEM((1, LANES), F32)],
        compiler_params=_cparams(("arbitrary",)),
    )(o_sb, o_g, p32, x_all, gnorm, w_out, nffn, w_r, b_r, ltri)


DMA_ISSUE_UNROLL = 8


def _dispatch_kernel(dest_ref, x_ref, xs_ref, sem, *, rows):
    def start(r, carry):
        for k in range(TOP_K):
            pltpu.make_async_copy(x_ref.at[pl.ds(r, 1), :],
                                  xs_ref.at[pl.ds(dest_ref[0, 0, r * TOP_K + k], 1), :], sem).start()
        return carry

    lax.fori_loop(0, rows, start, 0, unroll=DMA_ISSUE_UNROLL)
    for k in range(TOP_K):
        pltpu.make_async_copy(x_ref, xs_ref.at[pl.ds(0, rows), :], sem).wait()


def _dispatch(xp, dest, n_slots):
    n_tok, width = xp.shape
    rows = _pick_tile(n_tok, (256, 128))
    dest3 = dest.reshape(n_tok // rows, 1, rows * TOP_K)
    return pl.pallas_call(
        functools.partial(_dispatch_kernel, rows=rows),
        grid=(n_tok // rows,),
        in_specs=[pl.BlockSpec((1, 1, rows * TOP_K), lambda i: (i, 0, 0), memory_space=pltpu.SMEM),
                  pl.BlockSpec((rows, width), lambda i: (i, 0))],
        out_specs=pl.BlockSpec(memory_space=pl.ANY),
        out_shape=jax.ShapeDtypeStruct((n_slots, width), xp.dtype),
        scratch_shapes=[pltpu.SemaphoreType.DMA(())],
        compiler_params=_cparams(("arbitrary",)),
    )(dest3, xp)


def _unpack_bf16_pairs(packed):
    lo = pltpu.bitcast(packed << 16, F32).astype(BF16)
    hi = pltpu.bitcast(packed & jnp.uint32(0xFFFF0000), F32).astype(BF16)
    return jnp.concatenate([lo, hi], axis=1)


def _expert_changed(be_ref, m):
    prev = be_ref[jnp.maximum(m - 1, 0)]
    return (m == 0) | (be_ref[m] != prev)


def _moe_up_kernel(be_ref, nu_ref, xs_ref, wg_ref, wu_ref, bg_ref, bu_ref, h_ref, w16_ref, *, tf):
    m = pl.program_id(1)

    @pl.when(_expert_changed(be_ref, m))
    def _():
        w16_ref[:, :tf] = wg_ref[...].astype(BF16)
        w16_ref[:, tf:] = wu_ref[...].astype(BF16)

    @pl.when(m < nu_ref[0])
    def _():
        x = _unpack_bf16_pairs(xs_ref[...])
        gu = jnp.dot(x, w16_ref[...], preferred_element_type=F32)
        g = jnp.minimum(gu[:, :tf] + bg_ref[...], SWIGLU_LIMIT)
        u = jnp.clip(gu[:, tf:] + bu_ref[...], -SWIGLU_LIMIT, SWIGLU_LIMIT)
        h_ref[...] = ((u + 1.0) * (g * jax.nn.sigmoid(g * SWIGLU_ALPHA))).astype(h_ref.dtype)


def _moe_down_kernel(be_ref, nu_ref, h_ref, wd_ref, bd_ref, y_ref, w16_ref):
    m = pl.program_id(1)

    @pl.when(_expert_changed(be_ref, m))
    def _():
        w16_ref[...] = wd_ref[...].astype(BF16)

    @pl.when(m < nu_ref[0])
    def _():
        y_ref[...] = jnp.dot(h_ref[...], w16_ref[...], preferred_element_type=F32) + bd_ref[...]


def _moe_experts(xs, block_expert, n_used, w_gate_up, b_gate_up, w_down, b_down):
    n_slots = xs.shape[0]
    n_blocks = n_slots // MOE_TM
    tf = 512
    n_f = D_FF // tf
    live = lambda m, nu: jnp.minimum(m, nu[0] - 1)
    h = pl.pallas_call(
        functools.partial(_moe_up_kernel, tf=tf),
        grid_spec=pltpu.PrefetchScalarGridSpec(
            num_scalar_prefetch=2,
            grid=(n_f, n_blocks),
            in_specs=[
                pl.BlockSpec((MOE_TM, D_MODEL // 2), lambda j, m, be, nu: (live(m, nu), 0)),
                pl.BlockSpec((None, D_MODEL, tf), lambda j, m, be, nu: (be[m], 0, j)),
                pl.BlockSpec((None, D_MODEL, tf), lambda j, m, be, nu: (be[m], 0, n_f + j)),
                pl.BlockSpec((None, 1, tf), lambda j, m, be, nu: (be[m], 0, j)),
                pl.BlockSpec((None, 1, tf), lambda j, m, be, nu: (be[m], 0, n_f + j)),
            ],
            out_specs=pl.BlockSpec((MOE_TM, tf), lambda j, m, be, nu: (live(m, nu), j)),
            scratch_shapes=[pltpu.VMEM((D_MODEL, 2 * tf), BF16)],
        ),
        out_shape=jax.ShapeDtypeStruct((n_slots, D_FF), BF16),
        compiler_params=_cparams(("arbitrary", "arbitrary")),
    )(block_expert, n_used, xs, w_gate_up, w_gate_up, b_gate_up, b_gate_up)

    tn = 1024
    return pl.pallas_call(
        _moe_down_kernel,
        grid_spec=pltpu.PrefetchScalarGridSpec(
            num_scalar_prefetch=2,
            grid=(D_MODEL // tn, n_blocks),
            in_specs=[
                pl.BlockSpec((MOE_TM, D_FF), lambda j, m, be, nu: (live(m, nu), 0)),
                pl.BlockSpec((None, D_FF, tn), lambda j, m, be, nu: (be[m], 0, j)),
                pl.BlockSpec((None, 1, tn), lambda j, m, be, nu: (be[m], 0, j)),
            ],
            out_specs=pl.BlockSpec((MOE_TM, tn), lambda j, m, be, nu: (live(m, nu), j)),
            scratch_shapes=[pltpu.VMEM((D_FF, tn), BF16)],
        ),
        out_shape=jax.ShapeDtypeStruct((n_slots, D_MODEL), F32),
        compiler_params=_cparams(("arbitrary", "arbitrary")),
    )(block_expert, n_used, h, w_down, b_down)


def _combine_kernel(dest_ref, y_ref, gate_ref, h_ref, nf_ref, outp_ref, outs_ref, buf_ref, sem, *,
                    rows, prompt_tiles):
    def start(r, carry):
        for k in range(TOP_K):
            pltpu.make_async_copy(y_ref.at[pl.ds(dest_ref[0, 0, r * TOP_K + k], 1), :],
                                  buf_ref.at[k, pl.ds(r, 1), :], sem).start()
        return carry

    lax.fori_loop(0, rows, start, 0, unroll=DMA_ISSUE_UNROLL)
    for k in range(TOP_K):
        pltpu.make_async_copy(y_ref.at[pl.ds(0, rows), :], buf_ref.at[k], sem).wait()

    gate = gate_ref[...]
    moe = buf_ref[0] * gate[:, 0:1]
    for k in range(1, TOP_K):
        moe = moe + buf_ref[k] * gate[:, k:k + 1]
    h = h_ref[...] + moe
    inv = lax.rsqrt(jnp.mean(h * h, axis=-1, keepdims=True) + RMS_EPS)
    y = (h * inv) * nf_ref[...]
    is_prompt = pl.program_id(0) < prompt_tiles

    @pl.when(is_prompt)
    def _():
        outp_ref[...] = y

    @pl.when(jnp.logical_not(is_prompt))
    def _():
        outs_ref[...] = y


def _combine(y_slots, dest, gate, h, norm_final, n_prompt):
    n_tok = h.shape[0]
    rows = _pick_tile(math.gcd(n_prompt, n_tok - n_prompt), (256, 128))
    prompt_tiles = n_prompt // rows
    dest3 = dest.reshape(n_tok // rows, 1, rows * TOP_K)
    return pl.pallas_call(
        functools.partial(_combine_kernel, rows=rows, prompt_tiles=prompt_tiles),
        grid=(n_tok // rows,),
        in_specs=[pl.BlockSpec((1, 1, rows * TOP_K), lambda i: (i, 0, 0), memory_space=pltpu.SMEM),
                  pl.BlockSpec(memory_space=pl.ANY),
                  pl.BlockSpec((rows, LANES), lambda i: (i, 0)),
                  pl.BlockSpec((rows, D_MODEL), lambda i: (i, 0)),
                  pl.BlockSpec((1, D_MODEL), lambda i: (0, 0))],
        out_specs=[pl.BlockSpec((rows, D_MODEL), lambda i: (jnp.minimum(i, prompt_tiles - 1), 0)),
                   pl.BlockSpec((rows, D_MODEL), lambda i: (jnp.maximum(i - prompt_tiles, 0), 0))],
        out_shape=[jax.ShapeDtypeStruct((n_prompt, D_MODEL), F32),
                   jax.ShapeDtypeStruct((n_tok - n_prompt, D_MODEL), F32)],
        scratch_shapes=[pltpu.VMEM((TOP_K, rows, D_MODEL), F32), pltpu.SemaphoreType.DMA(())],
        compiler_params=_cparams(("arbitrary",)),
    )(dest3, y_slots, gate, h, norm_final)


def kernel(x_prompt, x_sample, cache_sb_k, cache_sb_v, state_gla, meta_tokens, norm_mix, w_in, w_gk2, b_gk,
           gla_norm, w_out, norm_ffn, w_router, b_router, w_gate_up, b_gate_up, w_down, b_down, norm_final):
    assert x_prompt.shape[0] == 1 and norm_mix.shape[0] == 1, "one prompt stream, one layer"
    n_prompt = x_prompt.shape[1]
    n_batch, n_new = x_sample.shape[:2]
    n_past = cache_sb_k.shape[2]
    assert n_new == GLA_CHUNK and n_prompt % SB_TQ == 0 and n_past % (SB_TK * SB_CACHE_SPLIT) == 0
    n_tok = n_prompt + n_batch * n_new
    assert n_tok % META_BLOCK == 0

    x_all = jnp.concatenate([
        x_prompt[0], x_sample.reshape(n_batch * n_new, D_MODEL), meta_tokens.astype(F32),
        jnp.zeros((META_BLOCK - N_META, D_MODEL), F32)], axis=0)

    w_in0 = w_in[0]
    w_main = w_in0[:, :D_MAIN].astype(BF16)
    w_a = jnp.pad(w_in0[:, D_MAIN:], ((0, 0), (0, LANES - GLA_RANK))).astype(BF16)
    w_gk = jnp.pad(w_gk2[0], ((0, LANES - GLA_RANK), (0, 0)))
    colscale = np.ones((1, D_MAIN), np.float32)
    colscale[0, :SB_WIDTH] = SB_DIM ** -0.5 * LOG2E
    colscale[0, 3 * SB_WIDTH:3 * SB_WIDTH + GLA_KEY_WIDTH] = GLA_DK ** -0.5
    p32, p16, lg = _inproj(x_all, norm_mix, w_main, jnp.asarray(colscale), w_a, w_gk, b_gk)

    u = jnp.asarray(np.tril(np.ones((SB_TK, SB_TK), np.float32)), BF16)
    o_sb = _sb_prompt(p16, u, n_prompt, n_tok)
    o_sb = _sb_sample(p16, cache_sb_k[0].reshape(n_batch, n_past * SB_HEADS, SB_DIM),
                      cache_sb_v[0].reshape(n_batch, n_past * SB_HEADS, SB_DIM), u, o_sb, n_prompt, n_batch, n_new)

    tri, sel = _gla_constants()
    c = GLA_CHUNK
    zero_state = jnp.zeros((1, GLA_HEADS, GLA_DK, GLA_DV), F32)
    _, state_meta = _gla(p32, lg, zero_state, tri, sel, c, first_chunk=n_tok // c, n_seq_chunks=1,
                         chained=True, n_valid=N_META)
    o_g, state_p = _gla(p32, lg, state_meta, tri, sel, n_tok, first_chunk=0, n_seq_chunks=n_prompt // c,
                        chained=True)
    o_g, state_s = _gla(p32, lg, state_gla[0].astype(F32), tri, sel, o_g, first_chunk=n_prompt // c,
                        n_seq_chunks=n_batch, chained=False)

    w_r = jnp.pad(w_router[0].astype(F32), ((0, 0), (0, LANES - N_EXPERTS)))
    w_r_hi = w_r.astype(BF16)
    w_r = jnp.concatenate([w_r_hi, (w_r - w_r_hi.astype(F32)).astype(BF16)], axis=1)
    b_r = jnp.pad(b_router.astype(F32), ((0, 0), (0, LANES - N_EXPERTS)), constant_values=-jnp.inf)
    h, xp, top_idx, gate, rank, counts = _mix(o_sb, o_g, p32, x_all, gla_norm, w_out[0].astype(BF16),
                                              norm_ffn, w_r, b_r, n_tok)

    counts = counts[0, :N_EXPERTS].astype(jnp.int32)
    padded = (counts + MOE_TM - 1) // MOE_TM * MOE_TM
    ends = jnp.cumsum(padded)
    n_blocks = -(-n_tok * TOP_K // MOE_TM) + N_EXPERTS
    dest = ((ends - padded)[top_idx[:, :TOP_K]] + rank[:, :TOP_K]).astype(jnp.int32)
    block_start = jnp.arange(n_blocks, dtype=jnp.int32) * MOE_TM
    block_expert = jnp.minimum(jnp.sum((ends[None, :] <= block_start[:, None]).astype(jnp.int32), axis=1),
                               N_EXPERTS - 1)
    n_used = (ends[-1:] // MOE_TM).astype(jnp.int32)

    xs = _dispatch(xp, dest, n_blocks * MOE_TM)
    y_slots = _moe_experts(xs, block_expert, n_used, w_gate_up[0], b_gate_up[0][:, None, :],
                           w_down[0], b_down[0][:, None, :])
    y_prompt, y_sample = _combine(y_slots, dest, gate, h, norm_final[None, :], n_prompt)
    y_prompt = y_prompt[None]
    y_sample = y_sample.reshape(n_batch, n_new, D_MODEL)
    k_lo, v_lo = SB_WIDTH, 2 * SB_WIDTH

    def kv_prompt(lo):
        rows = jnp.concatenate([p32[n_tok:n_tok + N_META, lo:lo + SB_WIDTH], p32[:n_prompt, lo:lo + SB_WIDTH]])
        return rows.reshape(1, 1, N_META + n_prompt, SB_HEADS, SB_DIM)

    def kv_sample(lo):
        return p32[n_prompt:n_tok, lo:lo + SB_WIDTH].reshape(1, n_batch, n_new, SB_HEADS, SB_DIM)

    return (y_prompt, y_sample, kv_prompt(k_lo), kv_prompt(v_lo), state_p[None],
            kv_sample(k_lo), kv_sample(v_lo), state_s[None])
```

```python
import functools
import math

import jax
import jax.numpy as jnp
import numpy as np
from jax import lax
from jax.experimental import pallas as pl
from jax.experimental.pallas import tpu as pltpu

F32 = jnp.float32
BF16 = jnp.bfloat16
HIGHEST = lax.Precision.HIGHEST

D_MODEL = 2048
N_META = 16
META_BLOCK = 128
SB_HEADS = 8
SB_DIM = 128
SB_WIDTH = SB_HEADS * SB_DIM
GLA_HEADS = 4
GLA_DK = 128
GLA_DV = 256
GLA_KEY_WIDTH = GLA_HEADS * GLA_DK
GLA_WIDTH = GLA_HEADS * GLA_DV
GLA_RANK = 16
GLA_TAU = 16.0
GLA_CHUNK = 64
N_EXPERTS = 32
TOP_K = 4
D_FF = 2048
SWIGLU_LIMIT = 7.0
SWIGLU_ALPHA = 1.702
RMS_EPS = 1e-5
D_MAIN = 3 * SB_WIDTH + 2 * GLA_KEY_WIDTH + 2 * GLA_WIDTH
LANES = 128
LOG2E = 1.4426950408889634

SB_TQ = 512
SB_TK = 256
SB_STEPS_PER_TRIP = 4
MOE_TM = 512
VMEM_LIMIT = 56 * 1024 * 1024


def _cparams(sem):
    return pltpu.CompilerParams(dimension_semantics=sem, vmem_limit_bytes=VMEM_LIMIT)


def _pick_tile(n, cands):
    for c in cands:
        if n % c == 0:
            return c
    raise ValueError(f"no tile for {n} in {cands}")


def _softplus(z):
    return jnp.maximum(z, 0.0) + jnp.log(1.0 + jnp.exp(-jnp.abs(z)))


def _inproj_kernel(x_ref, gain_ref, w_ref, scale_ref, wa_ref, wgk_ref, bgk_ref,
                   p32_ref, p16_ref, lg_ref, xn_ref):
    @pl.when(pl.program_id(1) == 0)
    def _():
        x = x_ref[...]
        inv = lax.rsqrt(jnp.mean(x * x, axis=-1, keepdims=True) + RMS_EPS)
        xn = ((x * inv) * gain_ref[...]).astype(BF16)
        xn_ref[...] = xn
        a = jnp.dot(xn, wa_ref[...], preferred_element_type=F32)
        pre = jnp.dot(a, wgk_ref[...], preferred_element_type=F32, precision=HIGHEST) + bgk_ref[...]
        lg_ref[...] = -_softplus(-pre) * (1.0 / GLA_TAU)

    y = jnp.dot(xn_ref[...], w_ref[...], preferred_element_type=F32) * scale_ref[...]
    p32_ref[...] = y
    p16_ref[...] = y.astype(BF16)


def _inproj(x_all, gain, w_main, colscale, w_a, w_gk, b_gk):
    rows = x_all.shape[0]
    tm = _pick_tile(rows, (640, 512, 256, 128))
    tn = 1024
    return pl.pallas_call(
        _inproj_kernel,
        grid=(rows // tm, D_MAIN // tn),
        in_specs=[
            pl.BlockSpec((tm, D_MODEL), lambda i, j: (i, 0)),
            pl.BlockSpec((1, D_MODEL), lambda i, j: (0, 0)),
            pl.BlockSpec((D_MODEL, tn), lambda i, j: (0, j)),
            pl.BlockSpec((1, tn), lambda i, j: (0, j)),
            pl.BlockSpec((D_MODEL, LANES), lambda i, j: (0, 0)),
            pl.BlockSpec((LANES, GLA_KEY_WIDTH), lambda i, j: (0, 0)),
            pl.BlockSpec((1, GLA_KEY_WIDTH), lambda i, j: (0, 0)),
        ],
        out_specs=[
            pl.BlockSpec((tm, tn), lambda i, j: (i, j)),
            pl.BlockSpec((tm, tn), lambda i, j: (i, j)),
            pl.BlockSpec((tm, GLA_KEY_WIDTH), lambda i, j: (i, 0)),
        ],
        out_shape=[
            jax.ShapeDtypeStruct((rows, D_MAIN), F32),
            jax.ShapeDtypeStruct((rows, D_MAIN), BF16),
            jax.ShapeDtypeStruct((rows, GLA_KEY_WIDTH), F32),
        ],
        scratch_shapes=[pltpu.VMEM((tm, D_MODEL), BF16)],
        compiler_params=_cparams(("arbitrary", "arbitrary")),
    )(x_all, gain, w_main, colscale, w_a, w_gk, b_gk)


def _sb_softplus2(z, allowed):
    neg_abs = pltpu.bitcast(pltpu.bitcast(z, jnp.uint32) | jnp.uint32(0x80000000), F32)
    sp = jnp.maximum(z, 0.0) + jnp.log2(1.0 + jnp.exp2(neg_abs))
    if allowed is not None:
        sp = jnp.where(allowed, sp, 0.0)
    return sp


def _sb_mass(z, u, allowed):
    return jnp.dot(_sb_softplus2(z, allowed).astype(BF16), u, preferred_element_type=F32)


def _sb_weight(z, r, c, allowed):
    w = jnp.exp2(z - r - c)
    if allowed is not None:
        w = jnp.where(allowed, w, 0.0)
    return w.astype(BF16)


def _sb_visit_many(qs, kblks, vblks, u, cs, alloweds):
    tk = u.shape[0]
    n = len(alloweds)
    zs = [lax.dot_general(q, kblk, (((1,), (1,)), ((), ())), preferred_element_type=F32)
          for q, kblk in zip(qs, kblks)]
    rs = [[_sb_mass(z[:, s * tk:(s + 1) * tk], u, alloweds[s]) for s in range(n)] for z in zs]
    ws, c_out = [], []
    for z, r, c in zip(zs, rs, cs):
        parts = [None] * n
        for s in reversed(range(n)):
            parts[s] = _sb_weight(z[:, s * tk:(s + 1) * tk], r[s], c, alloweds[s])
            c = c + r[s][:, 0:1]
        ws.append(parts[0] if n == 1 else jnp.concatenate(parts, axis=1))
        c_out.append(c)
    pvs = [jnp.dot(w, vblk, preferred_element_type=F32) for w, vblk in zip(ws, vblks)]
    return list(zip(pvs, c_out))


def _sb_visit(q, kblk, vblk, u, c, alloweds):
    return _sb_visit_many([q], [kblk], [vblk], u, [c], alloweds)[0]


def _sb_prompt_kernel(q_ref, k_ref, v_ref, km_ref, vm_ref, u_ref, o_ref,
                      acc_ref, c_ref, z_ref, sp_ref, t_ref, tot_ref):
    i = pl.program_id(1)
    q = q_ref[pl.ds(pl.multiple_of(i * SB_TQ, SB_TQ), SB_TQ), :]
    u = u_ref[...]
    n_sub = SB_TQ // SB_TK

    def key_rows(blk):
        return pl.ds(pl.multiple_of(blk * SB_TQ, SB_TQ), SB_TQ)

    def score_matmul(blk):
        return lax.dot_general(q, k_ref[key_rows(blk), :], (((1,), (1,)), ((), ())),
                               preferred_element_type=F32)

    def stage_softplus(z, slot, alloweds):
        z_ref[slot] = z
        for s in range(n_sub):
            cols = slice(s * SB_TK, (s + 1) * SB_TK)
            sp_ref[slot, :, cols] = _sb_softplus2(z[:, cols], alloweds[s]).astype(BF16)

    def stage_mass(slot, alloweds):
        for s in range(n_sub):
            cols = slice(s * SB_TK, (s + 1) * SB_TK)
            r_s = jnp.dot(sp_ref[slot, :, cols], u, preferred_element_type=F32)
            t_s = z_ref[slot, :, cols] - r_s
            if alloweds[s] is not None:
                t_s = jnp.where(alloweds[s], t_s, -jnp.inf)
            t_ref[:, cols] = t_s
            tot_ref[s] = r_s[:, 0:1]

    def stage_accumulate(blk):
        c = c_ref[...]
        ws = [None] * n_sub
        for s in reversed(range(n_sub)):
            ws[s] = jnp.exp2(t_ref[:, s * SB_TK:(s + 1) * SB_TK] - c).astype(BF16)
            c = c + tot_ref[s]
        acc_ref[...] += jnp.dot(jnp.concatenate(ws, axis=1), v_ref[key_rows(blk), :],
                                preferred_element_type=F32)
        c_ref[...] = c

    acc_ref[...] = jnp.zeros_like(acc_ref)
    c_ref[...] = jnp.zeros_like(c_ref)
    row_i = lax.broadcasted_iota(jnp.int32, (SB_TQ, SB_TK), 0)
    col_i = lax.broadcasted_iota(jnp.int32, (SB_TQ, SB_TK), 1)
    own = [col_i + s * SB_TK < row_i for s in range(n_sub)]
    free = [None] * n_sub
    stage_softplus(score_matmul(i), 0, own)
    stage_mass(0, own)
    stage_softplus(score_matmul(jnp.maximum(i - 1, 0)), 1, free)

    def step(p, slot):
        stage_softplus(score_matmul(i - p - 2), slot, free)
        stage_accumulate(i - p)
        stage_mass(1 - slot, free)

    def body(n, carry):
        for k in range(SB_STEPS_PER_TRIP):
            step(SB_STEPS_PER_TRIP * n + k, k % 2)
        return carry

    n_full = jnp.maximum(i - 1, 0)
    lax.fori_loop(0, n_full // SB_STEPS_PER_TRIP, body, 0)
    done = n_full - n_full % SB_STEPS_PER_TRIP
    for k in range(SB_STEPS_PER_TRIP - 1):
        @pl.when(n_full - done > k)
        def _():
            step(done + k, k % 2)

    @pl.when(i >= 1)
    def _():
        stage_accumulate(1)
        stage_mass(i % 2, free)

    stage_accumulate(0)

    meta_cols = lax.broadcasted_iota(jnp.int32, (SB_TQ, META_BLOCK), 1)
    pv, _ = _sb_visit(q, km_ref[...], vm_ref[...], u[:META_BLOCK, :META_BLOCK], c_ref[...],
                      [meta_cols < N_META])
    o_ref[...] = (acc_ref[...] + pv).astype(o_ref.dtype)


def _sb_prompt(p16, u, n_prompt, n_tok):
    head_spec = lambda off: pl.BlockSpec((n_prompt, SB_DIM), lambda h, i: (0, off + h))
    meta_spec = lambda off: pl.BlockSpec((META_BLOCK, SB_DIM), lambda h, i: (n_tok // META_BLOCK, off + h))
    return pl.pallas_call(
        _sb_prompt_kernel,
        grid=(SB_HEADS, n_prompt // SB_TQ),
        in_specs=[head_spec(0), head_spec(SB_HEADS), head_spec(2 * SB_HEADS),
                  meta_spec(SB_HEADS), meta_spec(2 * SB_HEADS),
                  pl.BlockSpec((SB_TK, SB_TK), lambda h, i: (0, 0))],
        out_specs=pl.BlockSpec((SB_TQ, SB_DIM), lambda h, i: (i, h)),
        out_shape=jax.ShapeDtypeStruct((n_tok, SB_WIDTH), BF16),
        scratch_shapes=[pltpu.VMEM((SB_TQ, SB_DIM), F32), pltpu.VMEM((SB_TQ, 1), F32),
                        pltpu.VMEM((2, SB_TQ, SB_TQ), F32), pltpu.VMEM((2, SB_TQ, SB_TQ), BF16),
                        pltpu.VMEM((SB_TQ, SB_TQ), F32), pltpu.VMEM((SB_TQ // SB_TK, SB_TQ, 1), F32)],
        compiler_params=_cparams(("arbitrary", "arbitrary")),
    )(p16, p16, p16, p16, p16, u)


SB_SAMPLE_GROUP = 8
SB_CACHE_SPLIT = 2


def _sb_sample_kernel(q_ref, kn_ref, vn_ref, kc_ref, vc_ref, u_ref, o_prev_ref, o_ref, acc_ref, c_ref, *,
                      n_new, n_pos):
    first = pl.program_id(1) == 0

    @pl.when(first & (pl.program_id(0) == 0))
    def _():
        acc_ref[...] = jnp.zeros_like(acc_ref)
        c_ref[...] = jnp.zeros_like(c_ref)

    u = u_ref[...]
    r_i = lax.broadcasted_iota(jnp.int32, (n_new, n_new), 0)
    c_i = lax.broadcasted_iota(jnp.int32, (n_new, n_new), 1)
    cols = [slice(head * SB_DIM, (head + 1) * SB_DIM) for head in range(SB_HEADS)]
    for group in range(0, SB_HEADS, SB_SAMPLE_GROUP):
        heads = range(group, group + SB_SAMPLE_GROUP)
        qs = [q_ref[:, cols[head]] for head in heads]
        new = _sb_visit_many(qs, [kn_ref[:, cols[head]] for head in heads],
                             [vn_ref[:, cols[head]] for head in heads], u[:n_new, :n_new],
                             [jnp.zeros((n_new, 1), F32)] * SB_SAMPLE_GROUP, [c_i < r_i])
        accs = [jnp.where(first, pv, acc_ref[head]) for head, (pv, _) in zip(heads, new)]
        cs = [jnp.where(first, tot, c_ref[head]) for head, (_, tot) in zip(heads, new)]
        kblks = [kc_ref[pl.ds(head, n_pos, stride=SB_HEADS), :].astype(BF16) for head in heads]
        vblks = [vc_ref[pl.ds(head, n_pos, stride=SB_HEADS), :].astype(BF16) for head in heads]
        old = _sb_visit_many(qs, kblks, vblks, u, cs, [None] * (n_pos // SB_TK))
        for head, acc0, (pv, c) in zip(heads, accs, old):
            acc = acc0 + pv
            acc_ref[head] = acc
            c_ref[head] = c
            o_ref[:, cols[head]] = acc.astype(o_ref.dtype)


def _sb_sample(p16, cache_k, cache_v, u, o_all, n_prompt, n_batch, n_new):
    n_past = cache_k.shape[1] // SB_HEADS
    n_pos = n_past // SB_CACHE_SPLIT
    assert n_pos % SB_TK == 0
    row0 = n_prompt // n_new
    new_spec = lambda blk: pl.BlockSpec((n_new, SB_WIDTH), lambda b, j: (row0 + b, blk))
    cache_spec = pl.BlockSpec((None, n_pos * SB_HEADS, SB_DIM), lambda b, j: (b, SB_CACHE_SPLIT - 1 - j, 0))
    return pl.pallas_call(
        functools.partial(_sb_sample_kernel, n_new=n_new, n_pos=n_pos),
        grid=(n_batch, SB_CACHE_SPLIT),
        in_specs=[new_spec(0), new_spec(1), new_spec(2), cache_spec, cache_spec,
                  pl.BlockSpec((SB_TK, SB_TK), lambda b, j: (0, 0)),
                  pl.BlockSpec(memory_space=pl.ANY)],
        out_specs=pl.BlockSpec((n_new, SB_WIDTH), lambda b, j: (row0 + b, 0)),
        out_shape=jax.ShapeDtypeStruct(o_all.shape, o_all.dtype),
        input_output_aliases={6: 0},
        scratch_shapes=[pltpu.VMEM((SB_HEADS, n_new, SB_DIM), F32), pltpu.VMEM((SB_HEADS, n_new, 1), F32)],
        compiler_params=_cparams(("arbitrary", "arbitrary")),
    )(p16, p16, p16, cache_k, cache_v, u, o_all)


GLA_LEVELS = 6


def _gla_constants():
    c = GLA_CHUNK
    tri = np.tril(np.ones((c, c), np.float32))
    sel = np.zeros((GLA_LEVELS * c, c), np.float32)
    for lvl in range(GLA_LEVELS):
        n = c >> lvl
        for row in range(c):
            sel[lvl * c + row, (row // n) * n + n // 2 - 1] = 1.0
    return jnp.asarray(tri), jnp.asarray(sel, BF16)


def _select_rows_exact(sel, x):
    width = x.shape[1]
    p1 = x.astype(BF16)
    rem = x - p1.astype(F32)
    p2 = rem.astype(BF16)
    p3 = (rem - p2.astype(F32)).astype(BF16)
    out = jnp.dot(sel, jnp.concatenate([p1, p2, p3], axis=1), preferred_element_type=F32)
    return (out[:, :width] + out[:, width:2 * width]) + out[:, 2 * width:]


def _gla_intra(items, tri, sel):
    c = GLA_CHUNK
    n = len(items)
    t_row = lax.broadcasted_iota(jnp.int32, (c, 1), 0)
    t_idx = lax.broadcasted_iota(jnp.int32, (c, c), 0)
    s_idx = lax.broadcasted_iota(jnp.int32, (c, c), 1)
    qk = lambda a, bb: lax.dot_general(a.astype(BF16), bb.astype(BF16), (((1,), (1,)), ((), ())),
                                       preferred_element_type=F32)
    bs = [jnp.dot(tri, lg, preferred_element_type=F32, precision=HIGHEST) for _, _, _, lg in items]
    refs = [_select_rows_exact(sel, b) for b in bs]
    atts = [jnp.where(t_idx == s_idx, qk(q, k), 0.0) for q, k, _, _ in items]
    for lvl in range(GLA_LEVELS):
        shift = GLA_LEVELS - 1 - lvl
        late = ((t_row >> shift) & 1) == 1
        pair = ((t_idx >> (shift + 1)) == (s_idx >> (shift + 1))) \
            & (((t_idx >> shift) & 1) == 1) & (((s_idx >> shift) & 1) == 0)
        parts = []
        for i, (q, k, _, _) in enumerate(items):
            r = refs[i][lvl * c:(lvl + 1) * c, :]
            qa = jnp.where(late, q * jnp.exp(bs[i] - r), 0.0)
            ka = jnp.where(late, 0.0, k * jnp.exp(r - bs[i]))
            parts.append(qk(qa, ka))
        atts = [atts[i] + jnp.where(pair, parts[i], 0.0) for i in range(n)]
    o_intra = [jnp.dot(atts[i].astype(BF16), items[i][2].astype(BF16), preferred_element_type=F32)
               for i in range(n)]
    out = []
    for i, (q, k, v, _) in enumerate(items):
        b_last = bs[i][c - 1:c, :]
        kd = (k * jnp.exp(b_last - bs[i])).astype(BF16)
        upd = lax.dot_general(v.astype(BF16), kd, (((0,), (0,)), ((), ())), preferred_element_type=F32)
        out.append(((q * jnp.exp(bs[i])).astype(BF16), o_intra[i], jnp.exp(b_last), upd))
    return out


def _gla_kernel(q_ref, k_ref, v_ref, lg_ref, s_in_ref, tri_ref, sel_ref, *rest, n_chunks, chained, n_valid,
                aliased):
    o_ref, s_out_ref, st_ref = rest[1:] if aliased else rest
    step = pl.program_id(0)
    live = None
    if n_valid != GLA_CHUNK:
        live = lax.broadcasted_iota(jnp.int32, (GLA_CHUNK, 1), 0) < n_valid

    if chained:
        @pl.when(step == 0)
        def _():
            for head in range(GLA_HEADS):
                st_ref[head] = s_in_ref[0, head].T

    items = []
    for j in range(n_chunks):
        rows = slice(j * GLA_CHUNK, (j + 1) * GLA_CHUNK)
        for head in range(GLA_HEADS):
            ks = slice(head * GLA_DK, (head + 1) * GLA_DK)
            k = k_ref[rows, ks]
            lg = lg_ref[rows, ks]
            if live is not None:
                k = jnp.where(live, k, 0.0)
                lg = jnp.where(live, lg, 0.0)
            items.append((q_ref[rows, ks], k, v_ref[rows, head * GLA_DV:(head + 1) * GLA_DV], lg))
    intra = _gla_intra(items, tri_ref[...], sel_ref[...])

    states = [st_ref[head] for head in range(GLA_HEADS)] if chained else None
    for j in range(n_chunks):
        rows = slice(j * GLA_CHUNK, (j + 1) * GLA_CHUNK)
        for head in range(GLA_HEADS):
            qe, o_intra, decay, upd = intra[j * GLA_HEADS + head]
            st = states[head] if chained else s_in_ref[j, head].T
            o_ref[rows, head * GLA_DV:(head + 1) * GLA_DV] = o_intra + lax.dot_general(
                qe, st.astype(BF16), (((1,), (1,)), ((), ())), preferred_element_type=F32)
            st = st * decay + upd
            if chained:
                states[head] = st
            else:
                s_out_ref[j, head] = st.T

    if chained:
        for head in range(GLA_HEADS):
            st_ref[head] = states[head]

        @pl.when(step == pl.num_programs(0) - 1)
        def _():
            for head in range(GLA_HEADS):
                s_out_ref[0, head] = states[head].T


def _gla(p32, lg, state_in, tri, sel, o_all, *, first_chunk, n_seq_chunks, chained, n_valid=GLA_CHUNK):
    c = GLA_CHUNK
    per_step = _pick_tile(n_seq_chunks, (4, 2, 1))
    assert first_chunk % per_step == 0
    steps = n_seq_chunks // per_step
    row0 = first_chunk // per_step
    q_blk = 3 * SB_WIDTH // GLA_KEY_WIDTH
    v_blk = (3 * SB_WIDTH + 2 * GLA_KEY_WIDTH) // GLA_WIDTH
    n_state = 1 if chained else n_seq_chunks
    state_spec = pl.BlockSpec((1 if chained else per_step, GLA_HEADS, GLA_DK, GLA_DV),
                              (lambda s: (0, 0, 0, 0)) if chained else (lambda s: (s, 0, 0, 0)))
    rows = per_step * c
    args = [p32, p32, p32, lg, state_in, tri, sel]
    if isinstance(o_all, int):
        out_rows, out0, alias_spec, extra = o_all, 0, [], {}
    else:
        out_rows, out0 = o_all.shape[0], row0
        alias_spec, extra = [pl.BlockSpec(memory_space=pl.ANY)], dict(input_output_aliases={len(args): 0})
        args.append(o_all)
    return pl.pallas_call(
        functools.partial(_gla_kernel, n_chunks=per_step, chained=chained, n_valid=n_valid,
                          aliased=bool(alias_spec)),
        grid=(steps,),
        in_specs=[
            pl.BlockSpec((rows, GLA_KEY_WIDTH), lambda s: (row0 + s, q_blk)),
            pl.BlockSpec((rows, GLA_KEY_WIDTH), lambda s: (row0 + s, q_blk + 1)),
            pl.BlockSpec((rows, GLA_WIDTH), lambda s: (row0 + s, v_blk)),
            pl.BlockSpec((rows, GLA_KEY_WIDTH), lambda s: (row0 + s, 0)),
            state_spec,
            pl.BlockSpec(tri.shape, lambda s: (0, 0)),
            pl.BlockSpec(sel.shape, lambda s: (0, 0)),
        ] + alias_spec,
        out_specs=[pl.BlockSpec((rows, GLA_WIDTH), lambda s: (out0 + s, 0)), state_spec],
        out_shape=[jax.ShapeDtypeStruct((out_rows, GLA_WIDTH), F32),
                   jax.ShapeDtypeStruct((n_state, GLA_HEADS, GLA_DK, GLA_DV), F32)],
        scratch_shapes=[pltpu.VMEM((GLA_HEADS, GLA_DV, GLA_DK), F32)],
        compiler_params=_cparams(("arbitrary",)),
        **extra,
    )(*args)


def _mix_kernel(osb_ref, og_ref, gg_ref, x_ref, gnorm_ref, wout_ref, nffn_ref, wr_ref, br_ref, ltri_ref,
                h_ref, xp_ref, idx_ref, gate_ref, rank_ref, cnt_ref, carry_ref):
    step = pl.program_id(0)

    @pl.when(step == 0)
    def _():
        carry_ref[...] = jnp.zeros_like(carry_ref)

    parts = []
    for head in range(GLA_HEADS):
        vs = slice(head * GLA_DV, (head + 1) * GLA_DV)
        og = og_ref[:, vs]
        inv = lax.rsqrt(jnp.mean(og * og, axis=-1, keepdims=True) + RMS_EPS)
        g = gg_ref[:, vs]
        parts.append((((og * inv) * gnorm_ref[...]) * (g * jax.nn.sigmoid(g))).astype(BF16))
    mixed = jnp.dot(jnp.concatenate([osb_ref[...]] + parts, axis=1), wout_ref[...],
                    preferred_element_type=F32)
    h = x_ref[...] + mixed
    h_ref[...] = h

    inv = lax.rsqrt(jnp.mean(h * h, axis=-1, keepdims=True) + RMS_EPS)
    xn = (h * inv) * nffn_ref[...]
    half = D_MODEL // 2
    lo_bits = pltpu.bitcast(xn[:, :half].astype(BF16).astype(F32), jnp.uint32)
    hi_bits = pltpu.bitcast(xn[:, half:].astype(BF16).astype(F32), jnp.uint32)
    xp_ref[...] = (hi_bits & jnp.uint32(0xFFFF0000)) | (lo_bits >> 16)

    x_hi = xn.astype(BF16)
    x_lo = (xn - x_hi.astype(F32)).astype(BF16)
    both = jnp.dot(x_hi, wr_ref[...], preferred_element_type=F32)
    logits = ((both[:, :LANES] + both[:, LANES:])
              + jnp.dot(x_lo, wr_ref[:, :LANES], preferred_element_type=F32)) + br_ref[...]
    tm = logits.shape[0]
    lane = lax.broadcasted_iota(jnp.int32, (tm, LANES), 1)
    vals, sels = [], []
    idx_out = jnp.zeros((tm, LANES), jnp.int32)
    for k in range(TOP_K):
        m = jnp.max(logits, axis=-1, keepdims=True)
        idx = jnp.min(jnp.where(logits == m, lane, LANES), axis=-1, keepdims=True)
        sel = lane == idx
        vals.append(m)
        sels.append(sel)
        idx_out = jnp.where(lane == k, idx, idx_out)
        logits = jnp.where(sel, -jnp.inf, logits)
    exps = [jnp.exp(v - vals[0]) for v in vals]
    denom = exps[0] + exps[1] + exps[2] + exps[3]
    gate_out = jnp.zeros((tm, LANES), F32)
    for k in range(TOP_K):
        gate_out = jnp.where(lane == k, exps[k] / denom, gate_out)
    idx_ref[...] = idx_out
    gate_ref[...] = gate_out

    hot = jnp.zeros((tm, LANES), F32)
    for sel in sels:
        hot = jnp.where(sel, 1.0, hot)
    before = jnp.dot(ltri_ref[...], hot.astype(BF16), preferred_element_type=F32) + carry_ref[...]
    rank_out = jnp.zeros((tm, LANES), jnp.int32)
    for k in range(TOP_K):
        rk = jnp.sum(jnp.where(sels[k], before, 0.0), axis=-1, keepdims=True)
        rank_out = jnp.where(lane == k, rk.astype(jnp.int32), rank_out)
    rank_ref[...] = rank_out
    carry_ref[...] = carry_ref[...] + jnp.sum(hot, axis=0, keepdims=True)
    cnt_ref[...] = carry_ref[...]


def _mix(o_sb, o_g, p32, x_all, gnorm, w_out, nffn, w_r, b_r, n_tok):
    tm = _pick_tile(n_tok, (512, 256, 128))
    g_blk = (3 * SB_WIDTH + 2 * GLA_KEY_WIDTH + GLA_WIDTH) // GLA_WIDTH
    ltri = jnp.asarray(np.tril(np.ones((tm, tm), np.float32), -1), BF16)
    const = lambda shape: pl.BlockSpec(shape, lambda i: (0, 0))
    tile = lambda w: pl.BlockSpec((tm, w), lambda i: (i, 0))
    return pl.pallas_call(
        _mix_kernel,
        grid=(n_tok // tm,),
        in_specs=[tile(SB_WIDTH), tile(GLA_WIDTH),
                  pl.BlockSpec((tm, GLA_WIDTH), lambda i: (i, g_blk)),
                  tile(D_MODEL), const((1, GLA_DV)), const((D_MODEL, D_MODEL)), const((1, D_MODEL)),
                  const((D_MODEL, 2 * LANES)), const((1, LANES)), const((tm, tm))],
        out_specs=[tile(D_MODEL), tile(D_MODEL // 2), tile(LANES), tile(LANES), tile(LANES),
                   const((1, LANES))],
        out_shape=[jax.ShapeDtypeStruct((n_tok, D_MODEL), F32),
                   jax.ShapeDtypeStruct((n_tok, D_MODEL // 2), jnp.uint32),
                   jax.ShapeDtypeStruct((n_tok, LANES), jnp.int32),
                   jax.ShapeDtypeStruct((n_tok, LANES), F32),
                   jax.ShapeDtypeStruct((n_tok, LANES), jnp.int32),
                   jax.ShapeDtypeStruct((1, LANES), F32)],
        scratch_shapes=[pltpu.VMEM((1, LANES), F32)],
        compiler_params=_cparams(("arbitrary",)),
    )(o_sb, o_g, p32, x_all, gnorm, w_out, nffn, w_r, b_r, ltri)


DMA_ISSUE_UNROLL = 8


def _dispatch_kernel(dest_ref, x_ref, xs_ref, sem, *, rows):
    def start(r, carry):
        for k in range(TOP_K):
            pltpu.make_async_copy(x_ref.at[pl.ds(r, 1), :],
                                  xs_ref.at[pl.ds(dest_ref[0, 0, r * TOP_K + k], 1), :], sem).start()
        return carry

    lax.fori_loop(0, rows, start, 0, unroll=DMA_ISSUE_UNROLL)
    for k in range(TOP_K):
        pltpu.make_async_copy(x_ref, xs_ref.at[pl.ds(0, rows), :], sem).wait()


def _dispatch(xp, dest, n_slots):
    n_tok, width = xp.shape
    rows = _pick_tile(n_tok, (256, 128))
    dest3 = dest.reshape(n_tok // rows, 1, rows * TOP_K)
    return pl.pallas_call(
        functools.partial(_dispatch_kernel, rows=rows),
        grid=(n_tok // rows,),
        in_specs=[pl.BlockSpec((1, 1, rows * TOP_K), lambda i: (i, 0, 0), memory_space=pltpu.SMEM),
                  pl.BlockSpec((rows, width), lambda i: (i, 0))],
        out_specs=pl.BlockSpec(memory_space=pl.ANY),
        out_shape=jax.ShapeDtypeStruct((n_slots, width), xp.dtype),
        scratch_shapes=[pltpu.SemaphoreType.DMA(())],
        compiler_params=_cparams(("arbitrary",)),
    )(dest3, xp)


def _unpack_bf16_pairs(packed):
    lo = pltpu.bitcast(packed << 16, F32).astype(BF16)
    hi = pltpu.bitcast(packed & jnp.uint32(0xFFFF0000), F32).astype(BF16)
    return jnp.concatenate([lo, hi], axis=1)


def _expert_changed(be_ref, m):
    prev = be_ref[jnp.maximum(m - 1, 0)]
    return (m == 0) | (be_ref[m] != prev)


def _moe_up_kernel(be_ref, nu_ref, xs_ref, wg_ref, wu_ref, bg_ref, bu_ref, h_ref, w16_ref, *, tf):
    m = pl.program_id(1)

    @pl.when(_expert_changed(be_ref, m))
    def _():
        w16_ref[:, :tf] = wg_ref[...].astype(BF16)
        w16_ref[:, tf:] = wu_ref[...].astype(BF16)

    @pl.when(m < nu_ref[0])
    def _():
        x = _unpack_bf16_pairs(xs_ref[...])
        gu = jnp.dot(x, w16_ref[...], preferred_element_type=F32)
        g = jnp.minimum(gu[:, :tf] + bg_ref[...], SWIGLU_LIMIT)
        u = jnp.clip(gu[:, tf:] + bu_ref[...], -SWIGLU_LIMIT, SWIGLU_LIMIT)
        h_ref[...] = ((u + 1.0) * (g * jax.nn.sigmoid(g * SWIGLU_ALPHA))).astype(h_ref.dtype)


def _moe_down_kernel(be_ref, nu_ref, h_ref, wd_ref, bd_ref, y_ref, w16_ref):
    m = pl.program_id(1)

    @pl.when(_expert_changed(be_ref, m))
    def _():
        w16_ref[...] = wd_ref[...].astype(BF16)

    @pl.when(m < nu_ref[0])
    def _():
        y_ref[...] = jnp.dot(h_ref[...], w16_ref[...], preferred_element_type=F32) + bd_ref[...]


def _moe_experts(xs, block_expert, n_used, w_gate_up, b_gate_up, w_down, b_down):
    n_slots = xs.shape[0]
    n_blocks = n_slots // MOE_TM
    tf = 1024
    n_f = D_FF // tf
    live = lambda m, nu: jnp.minimum(m, nu[0] - 1)
    h = pl.pallas_call(
        functools.partial(_moe_up_kernel, tf=tf),
        grid_spec=pltpu.PrefetchScalarGridSpec(
            num_scalar_prefetch=2,
            grid=(n_f, n_blocks),
            in_specs=[
                pl.BlockSpec((MOE_TM, D_MODEL // 2), lambda j, m, be, nu: (live(m, nu), 0)),
                pl.BlockSpec((None, D_MODEL, tf), lambda j, m, be, nu: (be[m], 0, j)),
                pl.BlockSpec((None, D_MODEL, tf), lambda j, m, be, nu: (be[m], 0, n_f + j)),
                pl.BlockSpec((None, 1, tf), lambda j, m, be, nu: (be[m], 0, j)),
                pl.BlockSpec((None, 1, tf), lambda j, m, be, nu: (be[m], 0, n_f + j)),
            ],
            out_specs=pl.BlockSpec((MOE_TM, tf), lambda j, m, be, nu: (live(m, nu), j)),
            scratch_shapes=[pltpu.VMEM((D_MODEL, 2 * tf), BF16)],
        ),
        out_shape=jax.ShapeDtypeStruct((n_slots, D_FF), BF16),
        compiler_params=_cparams(("arbitrary", "arbitrary")),
    )(block_expert, n_used, xs, w_gate_up, w_gate_up, b_gate_up, b_gate_up)

    tn = 1024
    return pl.pallas_call(
        _moe_down_kernel,
        grid_spec=pltpu.PrefetchScalarGridSpec(
            num_scalar_prefetch=2,
            grid=(D_MODEL // tn, n_blocks),
            in_specs=[
                pl.BlockSpec((MOE_TM, D_FF), lambda j, m, be, nu: (live(m, nu), 0)),
                pl.BlockSpec((None, D_FF, tn), lambda j, m, be, nu: (be[m], 0, j)),
                pl.BlockSpec((None, 1, tn), lambda j, m, be, nu: (be[m], 0, j)),
            ],
            out_specs=pl.BlockSpec((MOE_TM, tn), lambda j, m, be, nu: (live(m, nu), j)),
            scratch_shapes=[pltpu.VMEM((D_FF, tn), BF16)],
        ),
        out_shape=jax.ShapeDtypeStruct((n_slots, D_MODEL), F32),
        compiler_params=_cparams(("arbitrary", "arbitrary")),
    )(block_expert, n_used, h, w_down, b_down)


def _combine_kernel(dest_ref, y_ref, gate_ref, h_ref, nf_ref, outp_ref, outs_ref, buf_ref, sem, *,
                    rows, prompt_tiles):
    def start(r, carry):
        for k in range(TOP_K):
            pltpu.make_async_copy(y_ref.at[pl.ds(dest_ref[0, 0, r * TOP_K + k], 1), :],
                                  buf_ref.at[k, pl.ds(r, 1), :], sem).start()
        return carry

    lax.fori_loop(0, rows, start, 0, unroll=DMA_ISSUE_UNROLL)
    for k in range(TOP_K):
        pltpu.make_async_copy(y_ref.at[pl.ds(0, rows), :], buf_ref.at[k], sem).wait()

    gate = gate_ref[...]
    moe = buf_ref[0] * gate[:, 0:1]
    for k in range(1, TOP_K):
        moe = moe + buf_ref[k] * gate[:, k:k + 1]
    h = h_ref[...] + moe
    inv = lax.rsqrt(jnp.mean(h * h, axis=-1, keepdims=True) + RMS_EPS)
    y = (h * inv) * nf_ref[...]
    is_prompt = pl.program_id(0) < prompt_tiles

    @pl.when(is_prompt)
    def _():
        outp_ref[...] = y

    @pl.when(jnp.logical_not(is_prompt))
    def _():
        outs_ref[...] = y


def _combine(y_slots, dest, gate, h, norm_final, n_prompt):
    n_tok = h.shape[0]
    rows = _pick_tile(math.gcd(n_prompt, n_tok - n_prompt), (256, 128))
    prompt_tiles = n_prompt // rows
    dest3 = dest.reshape(n_tok // rows, 1, rows * TOP_K)
    return pl.pallas_call(
        functools.partial(_combine_kernel, rows=rows, prompt_tiles=prompt_tiles),
        grid=(n_tok // rows,),
        in_specs=[pl.BlockSpec((1, 1, rows * TOP_K), lambda i: (i, 0, 0), memory_space=pltpu.SMEM),
                  pl.BlockSpec(memory_space=pl.ANY),
                  pl.BlockSpec((rows, LANES), lambda i: (i, 0)),
                  pl.BlockSpec((rows, D_MODEL), lambda i: (i, 0)),
                  pl.BlockSpec((1, D_MODEL), lambda i: (0, 0))],
        out_specs=[pl.BlockSpec((rows, D_MODEL), lambda i: (jnp.minimum(i, prompt_tiles - 1), 0)),
                   pl.BlockSpec((rows, D_MODEL), lambda i: (jnp.maximum(i - prompt_tiles, 0), 0))],
        out_shape=[jax.ShapeDtypeStruct((n_prompt, D_MODEL), F32),
                   jax.ShapeDtypeStruct((n_tok - n_prompt, D_MODEL), F32)],
        scratch_shapes=[pltpu.VMEM((TOP_K, rows, D_MODEL), F32), pltpu.SemaphoreType.DMA(())],
        compiler_params=_cparams(("arbitrary",)),
    )(dest3, y_slots, gate, h, norm_final)


def kernel(x_prompt, x_sample, cache_sb_k, cache_sb_v, state_gla, meta_tokens, norm_mix, w_in, w_gk2, b_gk,
           gla_norm, w_out, norm_ffn, w_router, b_router, w_gate_up, b_gate_up, w_down, b_down, norm_final):
    assert x_prompt.shape[0] == 1 and norm_mix.shape[0] == 1, "one prompt stream, one layer"
    n_prompt = x_prompt.shape[1]
    n_batch, n_new = x_sample.shape[:2]
    n_past = cache_sb_k.shape[2]
    assert n_new == GLA_CHUNK and n_prompt % SB_TQ == 0 and n_past % (SB_TK * SB_CACHE_SPLIT) == 0
    n_tok = n_prompt + n_batch * n_new
    assert n_tok % META_BLOCK == 0

    x_all = jnp.concatenate([
        x_prompt[0], x_sample.reshape(n_batch * n_new, D_MODEL), meta_tokens.astype(F32),
        jnp.zeros((META_BLOCK - N_META, D_MODEL), F32)], axis=0)

    w_in0 = w_in[0]
    w_main = w_in0[:, :D_MAIN].astype(BF16)
    w_a = jnp.pad(w_in0[:, D_MAIN:], ((0, 0), (0, LANES - GLA_RANK))).astype(BF16)
    w_gk = jnp.pad(w_gk2[0], ((0, LANES - GLA_RANK), (0, 0)))
    colscale = np.ones((1, D_MAIN), np.float32)
    colscale[0, :SB_WIDTH] = SB_DIM ** -0.5 * LOG2E
    colscale[0, 3 * SB_WIDTH:3 * SB_WIDTH + GLA_KEY_WIDTH] = GLA_DK ** -0.5
    p32, p16, lg = _inproj(x_all, norm_mix, w_main, jnp.asarray(colscale), w_a, w_gk, b_gk)

    u = jnp.asarray(np.tril(np.ones((SB_TK, SB_TK), np.float32)), BF16)
    o_sb = _sb_prompt(p16, u, n_prompt, n_tok)
    o_sb = _sb_sample(p16, cache_sb_k[0].reshape(n_batch, n_past * SB_HEADS, SB_DIM),
                      cache_sb_v[0].reshape(n_batch, n_past * SB_HEADS, SB_DIM), u, o_sb, n_prompt, n_batch, n_new)

    tri, sel = _gla_constants()
    c = GLA_CHUNK
    zero_state = jnp.zeros((1, GLA_HEADS, GLA_DK, GLA_DV), F32)
    _, state_meta = _gla(p32, lg, zero_state, tri, sel, c, first_chunk=n_tok // c, n_seq_chunks=1,
                         chained=True, n_valid=N_META)
    o_g, state_p = _gla(p32, lg, state_meta, tri, sel, n_tok, first_chunk=0, n_seq_chunks=n_prompt // c,
                        chained=True)
    o_g, state_s = _gla(p32, lg, state_gla[0].astype(F32), tri, sel, o_g, first_chunk=n_prompt // c,
                        n_seq_chunks=n_batch, chained=False)

    w_r = jnp.pad(w_router[0].astype(F32), ((0, 0), (0, LANES - N_EXPERTS)))
    w_r_hi = w_r.astype(BF16)
    w_r = jnp.concatenate([w_r_hi, (w_r - w_r_hi.astype(F32)).astype(BF16)], axis=1)
    b_r = jnp.pad(b_router.astype(F32), ((0, 0), (0, LANES - N_EXPERTS)), constant_values=-jnp.inf)
    h, xp, top_idx, gate, rank, counts = _mix(o_sb, o_g, p32, x_all, gla_norm, w_out[0].astype(BF16),
                                              norm_ffn, w_r, b_r, n_tok)

    counts = counts[0, :N_EXPERTS].astype(jnp.int32)
    padded = (counts + MOE_TM - 1) // MOE_TM * MOE_TM
    ends = jnp.cumsum(padded)
    n_blocks = -(-n_tok * TOP_K // MOE_TM) + N_EXPERTS
    dest = ((ends - padded)[top_idx[:, :TOP_K]] + rank[:, :TOP_K]).astype(jnp.int32)
    block_start = jnp.arange(n_blocks, dtype=jnp.int32) * MOE_TM
    block_expert = jnp.minimum(jnp.sum((ends[None, :] <= block_start[:, None]).astype(jnp.int32), axis=1),
                               N_EXPERTS - 1)
    n_used = (ends[-1:] // MOE_TM).astype(jnp.int32)

    xs = _dispatch(xp, dest, n_blocks * MOE_TM)
    y_slots = _moe_experts(xs, block_expert, n_used, w_gate_up[0], b_gate_up[0][:, None, :],
                           w_down[0], b_down[0][:, None, :])
    y_prompt, y_sample = _combine(y_slots, dest, gate, h, norm_final[None, :], n_prompt)
    y_prompt = y_prompt[None]
    y_sample = y_sample.reshape(n_batch, n_new, D_MODEL)
    k_lo, v_lo = SB_WIDTH, 2 * SB_WIDTH

    def kv_prompt(lo):
        rows = jnp.concatenate([p32[n_tok:n_tok + N_META, lo:lo + SB_WIDTH], p32[:n_prompt, lo:lo + SB_WIDTH]])
        return rows.reshape(1, 1, N_META + n_prompt, SB_HEADS, SB_DIM)

    def kv_sample(lo):
        return p32[n_prompt:n_tok, lo:lo + SB_WIDTH].reshape(1, n_batch, n_new, SB_HEADS, SB_DIM)

    return (y_prompt, y_sample, kv_prompt(k_lo), kv_prompt(v_lo), state_p[None],
            kv_sample(k_lo), kv_sample(v_lo), state_s[None])
```

```python
import functools
import math

import jax
import jax.numpy as jnp
import numpy as np
from jax import lax
from jax.experimental import pallas as pl
from jax.experimental.pallas import tpu as pltpu

F32 = jnp.float32
BF16 = jnp.bfloat16
HIGHEST = lax.Precision.HIGHEST

D_MODEL = 2048
N_META = 16
META_BLOCK = 128
SB_HEADS = 8
SB_DIM = 128
SB_WIDTH = SB_HEADS * SB_DIM
GLA_HEADS = 4
GLA_DK = 128
GLA_DV = 256
GLA_KEY_WIDTH = GLA_HEADS * GLA_DK
GLA_WIDTH = GLA_HEADS * GLA_DV
GLA_RANK = 16
GLA_TAU = 16.0
GLA_CHUNK = 64
N_EXPERTS = 32
TOP_K = 4
D_FF = 2048
SWIGLU_LIMIT = 7.0
SWIGLU_ALPHA = 1.702
RMS_EPS = 1e-5
D_MAIN = 3 * SB_WIDTH + 2 * GLA_KEY_WIDTH + 2 * GLA_WIDTH
LANES = 128
LOG2E = 1.4426950408889634

SB_TQ = 512
SB_TK = 256
SB_STEPS_PER_TRIP = 4
MOE_TM = 512
VMEM_LIMIT = 58 * 1024 * 1024


def _cparams(sem):
    return pltpu.CompilerParams(dimension_semantics=sem, vmem_limit_bytes=VMEM_LIMIT)


def _pick_tile(n, cands):
    for c in cands:
        if n % c == 0:
            return c
    raise ValueError(f"no tile for {n} in {cands}")


def _softplus(z):
    return jnp.maximum(z, 0.0) + jnp.log(1.0 + jnp.exp(-jnp.abs(z)))


def _inproj_kernel(x_ref, gain_ref, w_ref, scale_ref, wa_ref, wgk_ref, bgk_ref,
                   p32_ref, p16_ref, lg_ref, xn_ref):
    @pl.when(pl.program_id(1) == 0)
    def _():
        x = x_ref[...]
        inv = lax.rsqrt(jnp.mean(x * x, axis=-1, keepdims=True) + RMS_EPS)
        xn = ((x * inv) * gain_ref[...]).astype(BF16)
        xn_ref[...] = xn
        a = jnp.dot(xn, wa_ref[...], preferred_element_type=F32)
        pre = jnp.dot(a, wgk_ref[...], preferred_element_type=F32, precision=HIGHEST) + bgk_ref[...]
        lg_ref[...] = -_softplus(-pre) * (1.0 / GLA_TAU)

    y = jnp.dot(xn_ref[...], w_ref[...], preferred_element_type=F32) * scale_ref[...]
    p32_ref[...] = y
    p16_ref[...] = y.astype(BF16)


def _inproj(x_all, gain, w_main, colscale, w_a, w_gk, b_gk):
    rows = x_all.shape[0]
    tm = _pick_tile(rows, (640, 512, 256, 128))
    tn = 1024
    return pl.pallas_call(
        _inproj_kernel,
        grid=(rows // tm, D_MAIN // tn),
        in_specs=[
            pl.BlockSpec((tm, D_MODEL), lambda i, j: (i, 0)),
            pl.BlockSpec((1, D_MODEL), lambda i, j: (0, 0)),
            pl.BlockSpec((D_MODEL, tn), lambda i, j: (0, j)),
            pl.BlockSpec((1, tn), lambda i, j: (0, j)),
            pl.BlockSpec((D_MODEL, LANES), lambda i, j: (0, 0)),
            pl.BlockSpec((LANES, GLA_KEY_WIDTH), lambda i, j: (0, 0)),
            pl.BlockSpec((1, GLA_KEY_WIDTH), lambda i, j: (0, 0)),
        ],
        out_specs=[
            pl.BlockSpec((tm, tn), lambda i, j: (i, j)),
            pl.BlockSpec((tm, tn), lambda i, j: (i, j)),
            pl.BlockSpec((tm, GLA_KEY_WIDTH), lambda i, j: (i, 0)),
        ],
        out_shape=[
            jax.ShapeDtypeStruct((rows, D_MAIN), F32),
            jax.ShapeDtypeStruct((rows, D_MAIN), BF16),
            jax.ShapeDtypeStruct((rows, GLA_KEY_WIDTH), F32),
        ],
        scratch_shapes=[pltpu.VMEM((tm, D_MODEL), BF16)],
        compiler_params=_cparams(("arbitrary", "arbitrary")),
    )(x_all, gain, w_main, colscale, w_a, w_gk, b_gk)


def _sb_softplus2(z, allowed):
    sp = jnp.maximum(z, 0.0) + jnp.log2(1.0 + jnp.exp2(-jnp.abs(z)))
    if allowed is not None:
        sp = jnp.where(allowed, sp, 0.0)
    return sp


def _sb_mass(z, u, allowed):
    return jnp.dot(_sb_softplus2(z, allowed).astype(BF16), u, preferred_element_type=F32)


def _sb_weight(z, r, c, allowed):
    w = jnp.exp2(z - r - c)
    if allowed is not None:
        w = jnp.where(allowed, w, 0.0)
    return w.astype(BF16)


def _sb_visit_many(qs, kblks, vblks, u, cs, alloweds):
    tk = u.shape[0]
    n = len(alloweds)
    zs = [lax.dot_general(q, kblk, (((1,), (1,)), ((), ())), preferred_element_type=F32)
          for q, kblk in zip(qs, kblks)]
    rs = [[_sb_mass(z[:, s * tk:(s + 1) * tk], u, alloweds[s]) for s in range(n)] for z in zs]
    ws, c_out = [], []
    for z, r, c in zip(zs, rs, cs):
        parts = [None] * n
        for s in reversed(range(n)):
            parts[s] = _sb_weight(z[:, s * tk:(s + 1) * tk], r[s], c, alloweds[s])
            c = c + r[s][:, 0:1]
        ws.append(parts[0] if n == 1 else jnp.concatenate(parts, axis=1))
        c_out.append(c)
    pvs = [jnp.dot(w, vblk, preferred_element_type=F32) for w, vblk in zip(ws, vblks)]
    return list(zip(pvs, c_out))


def _sb_visit(q, kblk, vblk, u, c, alloweds):
    return _sb_visit_many([q], [kblk], [vblk], u, [c], alloweds)[0]


def _sb_prompt_kernel(q_ref, k_ref, v_ref, km_ref, vm_ref, u_ref, o_ref,
                      acc_ref, c_ref, z_ref, sp_ref, t_ref, tot_ref):
    i = pl.program_id(1)
    q = q_ref[pl.ds(pl.multiple_of(i * SB_TQ, SB_TQ), SB_TQ), :]
    u = u_ref[...]
    n_sub = SB_TQ // SB_TK

    def key_rows(blk):
        return pl.ds(pl.multiple_of(blk * SB_TQ, SB_TQ), SB_TQ)

    def score_matmul(blk):
        return lax.dot_general(q, k_ref[key_rows(blk), :], (((1,), (1,)), ((), ())),
                               preferred_element_type=F32)

    def stage_softplus(z, slot, alloweds):
        z_ref[slot] = z
        for s in range(n_sub):
            cols = slice(s * SB_TK, (s + 1) * SB_TK)
            sp_ref[slot, :, cols] = _sb_softplus2(z[:, cols], alloweds[s]).astype(BF16)

    def stage_mass(slot, alloweds):
        for s in range(n_sub):
            cols = slice(s * SB_TK, (s + 1) * SB_TK)
            r_s = jnp.dot(sp_ref[slot, :, cols], u, preferred_element_type=F32)
            t_s = z_ref[slot, :, cols] - r_s
            if alloweds[s] is not None:
                t_s = jnp.where(alloweds[s], t_s, -jnp.inf)
            t_ref[:, cols] = t_s
            tot_ref[s] = r_s[:, 0:1]

    def stage_accumulate(blk):
        c = c_ref[...]
        ws = [None] * n_sub
        for s in reversed(range(n_sub)):
            ws[s] = jnp.exp2(t_ref[:, s * SB_TK:(s + 1) * SB_TK] - c).astype(BF16)
            c = c + tot_ref[s]
        acc_ref[...] += jnp.dot(jnp.concatenate(ws, axis=1), v_ref[key_rows(blk), :],
                                preferred_element_type=F32)
        c_ref[...] = c

    acc_ref[...] = jnp.zeros_like(acc_ref)
    c_ref[...] = jnp.zeros_like(c_ref)
    row_i = lax.broadcasted_iota(jnp.int32, (SB_TQ, SB_TK), 0)
    col_i = lax.broadcasted_iota(jnp.int32, (SB_TQ, SB_TK), 1)
    own = [col_i + s * SB_TK < row_i for s in range(n_sub)]
    free = [None] * n_sub
    stage_softplus(score_matmul(i), 0, own)
    stage_mass(0, own)
    stage_softplus(score_matmul(jnp.maximum(i - 1, 0)), 1, free)

    def step(p, slot):
        stage_softplus(score_matmul(i - p - 2), slot, free)
        stage_accumulate(i - p)
        stage_mass(1 - slot, free)

    def body(n, carry):
        for k in range(SB_STEPS_PER_TRIP):
            step(SB_STEPS_PER_TRIP * n + k, k % 2)
        return carry

    n_full = jnp.maximum(i - 1, 0)
    lax.fori_loop(0, n_full // SB_STEPS_PER_TRIP, body, 0)
    done = n_full - n_full % SB_STEPS_PER_TRIP
    for k in range(SB_STEPS_PER_TRIP - 1):
        @pl.when(n_full - done > k)
        def _():
            step(done + k, k % 2)

    @pl.when(i >= 1)
    def _():
        stage_accumulate(1)
        stage_mass(i % 2, free)

    stage_accumulate(0)

    meta_cols = lax.broadcasted_iota(jnp.int32, (SB_TQ, META_BLOCK), 1)
    pv, _ = _sb_visit(q, km_ref[...], vm_ref[...], u[:META_BLOCK, :META_BLOCK], c_ref[...],
                      [meta_cols < N_META])
    o_ref[...] = (acc_ref[...] + pv).astype(o_ref.dtype)


def _sb_prompt(p16, u, n_prompt, n_tok):
    head_spec = lambda off: pl.BlockSpec((n_prompt, SB_DIM), lambda h, i: (0, off + h))
    meta_spec = lambda off: pl.BlockSpec((META_BLOCK, SB_DIM), lambda h, i: (n_tok // META_BLOCK, off + h))
    return pl.pallas_call(
        _sb_prompt_kernel,
        grid=(SB_HEADS, n_prompt // SB_TQ),
        in_specs=[head_spec(0), head_spec(SB_HEADS), head_spec(2 * SB_HEADS),
                  meta_spec(SB_HEADS), meta_spec(2 * SB_HEADS),
                  pl.BlockSpec((SB_TK, SB_TK), lambda h, i: (0, 0))],
        out_specs=pl.BlockSpec((SB_TQ, SB_DIM), lambda h, i: (i, h)),
        out_shape=jax.ShapeDtypeStruct((n_tok, SB_WIDTH), BF16),
        scratch_shapes=[pltpu.VMEM((SB_TQ, SB_DIM), F32), pltpu.VMEM((SB_TQ, 1), F32),
                        pltpu.VMEM((2, SB_TQ, SB_TQ), F32), pltpu.VMEM((2, SB_TQ, SB_TQ), BF16),
                        pltpu.VMEM((SB_TQ, SB_TQ), F32), pltpu.VMEM((SB_TQ // SB_TK, SB_TQ, 1), F32)],
        compiler_params=_cparams(("arbitrary", "arbitrary")),
    )(p16, p16, p16, p16, p16, u)


SB_SAMPLE_GROUP = 8
SB_CACHE_SPLIT = 2


def _sb_sample_kernel(q_ref, kn_ref, vn_ref, kc_ref, vc_ref, u_ref, o_prev_ref, o_ref, acc_ref, c_ref, *,
                      n_new, n_pos):
    first = pl.program_id(1) == 0

    @pl.when(first & (pl.program_id(0) == 0))
    def _():
        acc_ref[...] = jnp.zeros_like(acc_ref)
        c_ref[...] = jnp.zeros_like(c_ref)

    u = u_ref[...]
    r_i = lax.broadcasted_iota(jnp.int32, (n_new, n_new), 0)
    c_i = lax.broadcasted_iota(jnp.int32, (n_new, n_new), 1)
    cols = [slice(head * SB_DIM, (head + 1) * SB_DIM) for head in range(SB_HEADS)]
    for group in range(0, SB_HEADS, SB_SAMPLE_GROUP):
        heads = range(group, group + SB_SAMPLE_GROUP)
        qs = [q_ref[:, cols[head]] for head in heads]
        new = _sb_visit_many(qs, [kn_ref[:, cols[head]] for head in heads],
                             [vn_ref[:, cols[head]] for head in heads], u[:n_new, :n_new],
                             [jnp.zeros((n_new, 1), F32)] * SB_SAMPLE_GROUP, [c_i < r_i])
        accs = [jnp.where(first, pv, acc_ref[head]) for head, (pv, _) in zip(heads, new)]
        cs = [jnp.where(first, tot, c_ref[head]) for head, (_, tot) in zip(heads, new)]
        kblks = [kc_ref[pl.ds(head, n_pos, stride=SB_HEADS), :].astype(BF16) for head in heads]
        vblks = [vc_ref[pl.ds(head, n_pos, stride=SB_HEADS), :].astype(BF16) for head in heads]
        old = _sb_visit_many(qs, kblks, vblks, u, cs, [None] * (n_pos // SB_TK))
        for head, acc0, (pv, c) in zip(heads, accs, old):
            acc = acc0 + pv
            acc_ref[head] = acc
            c_ref[head] = c
            o_ref[:, cols[head]] = acc.astype(o_ref.dtype)


def _sb_sample(p16, cache_k, cache_v, u, o_all, n_prompt, n_batch, n_new):
    n_past = cache_k.shape[1] // SB_HEADS
    n_pos = n_past // SB_CACHE_SPLIT
    assert n_pos % SB_TK == 0
    row0 = n_prompt // n_new
    new_spec = lambda blk: pl.BlockSpec((n_new, SB_WIDTH), lambda b, j: (row0 + b, blk))
    cache_spec = pl.BlockSpec((None, n_pos * SB_HEADS, SB_DIM), lambda b, j: (b, SB_CACHE_SPLIT - 1 - j, 0))
    return pl.pallas_call(
        functools.partial(_sb_sample_kernel, n_new=n_new, n_pos=n_pos),
        grid=(n_batch, SB_CACHE_SPLIT),
        in_specs=[new_spec(0), new_spec(1), new_spec(2), cache_spec, cache_spec,
                  pl.BlockSpec((SB_TK, SB_TK), lambda b, j: (0, 0)),
                  pl.BlockSpec(memory_space=pl.ANY)],
        out_specs=pl.BlockSpec((n_new, SB_WIDTH), lambda b, j: (row0 + b, 0)),
        out_shape=jax.ShapeDtypeStruct(o_all.shape, o_all.dtype),
        input_output_aliases={6: 0},
        scratch_shapes=[pltpu.VMEM((SB_HEADS, n_new, SB_DIM), F32), pltpu.VMEM((SB_HEADS, n_new, 1), F32)],
        compiler_params=_cparams(("arbitrary", "arbitrary")),
    )(p16, p16, p16, cache_k, cache_v, u, o_all)


GLA_LEVELS = 6


def _gla_constants():
    c = GLA_CHUNK
    tri = np.tril(np.ones((c, c), np.float32))
    sel = np.zeros((GLA_LEVELS * c, c), np.float32)
    for lvl in range(GLA_LEVELS):
        n = c >> lvl
        for row in range(c):
            sel[lvl * c + row, (row // n) * n + n // 2 - 1] = 1.0
    return jnp.asarray(tri), jnp.asarray(sel, BF16)


def _select_rows_exact(sel, x):
    width = x.shape[1]
    p1 = x.astype(BF16)
    rem = x - p1.astype(F32)
    p2 = rem.astype(BF16)
    p3 = (rem - p2.astype(F32)).astype(BF16)
    out = jnp.dot(sel, jnp.concatenate([p1, p2, p3], axis=1), preferred_element_type=F32)
    return (out[:, :width] + out[:, width:2 * width]) + out[:, 2 * width:]


def _gla_intra(items, tri, sel):
    c = GLA_CHUNK
    n = len(items)
    t_row = lax.broadcasted_iota(jnp.int32, (c, 1), 0)
    t_idx = lax.broadcasted_iota(jnp.int32, (c, c), 0)
    s_idx = lax.broadcasted_iota(jnp.int32, (c, c), 1)
    qk = lambda a, bb: lax.dot_general(a.astype(BF16), bb.astype(BF16), (((1,), (1,)), ((), ())),
                                       preferred_element_type=F32)
    bs = [jnp.dot(tri, lg, preferred_element_type=F32, precision=HIGHEST) for _, _, _, lg in items]
    refs = [_select_rows_exact(sel, b) for b in bs]
    atts = [jnp.where(t_idx == s_idx, qk(q, k), 0.0) for q, k, _, _ in items]
    for lvl in range(GLA_LEVELS):
        shift = GLA_LEVELS - 1 - lvl
        late = ((t_row >> shift) & 1) == 1
        pair = ((t_idx >> (shift + 1)) == (s_idx >> (shift + 1))) \
            & (((t_idx >> shift) & 1) == 1) & (((s_idx >> shift) & 1) == 0)
        parts = []
        for i, (q, k, _, _) in enumerate(items):
            r = refs[i][lvl * c:(lvl + 1) * c, :]
            qa = jnp.where(late, q * jnp.exp(bs[i] - r), 0.0)
            ka = jnp.where(late, 0.0, k * jnp.exp(r - bs[i]))
            parts.append(qk(qa, ka))
        atts = [atts[i] + jnp.where(pair, parts[i], 0.0) for i in range(n)]
    o_intra = [jnp.dot(atts[i].astype(BF16), items[i][2].astype(BF16), preferred_element_type=F32)
               for i in range(n)]
    out = []
    for i, (q, k, v, _) in enumerate(items):
        b_last = bs[i][c - 1:c, :]
        kd = (k * jnp.exp(b_last - bs[i])).astype(BF16)
        upd = lax.dot_general(v.astype(BF16), kd, (((0,), (0,)), ((), ())), preferred_element_type=F32)
        out.append(((q * jnp.exp(bs[i])).astype(BF16), o_intra[i], jnp.exp(b_last), upd))
    return out


def _gla_kernel(q_ref, k_ref, v_ref, lg_ref, s_in_ref, tri_ref, sel_ref, *rest, n_chunks, chained, n_valid,
                aliased):
    o_ref, s_out_ref, st_ref = rest[1:] if aliased else rest
    step = pl.program_id(0)
    live = None
    if n_valid != GLA_CHUNK:
        live = lax.broadcasted_iota(jnp.int32, (GLA_CHUNK, 1), 0) < n_valid

    if chained:
        @pl.when(step == 0)
        def _():
            for head in range(GLA_HEADS):
                st_ref[head] = s_in_ref[0, head].T

    items = []
    for j in range(n_chunks):
        rows = slice(j * GLA_CHUNK, (j + 1) * GLA_CHUNK)
        for head in range(GLA_HEADS):
            ks = slice(head * GLA_DK, (head + 1) * GLA_DK)
            k = k_ref[rows, ks]
            lg = lg_ref[rows, ks]
            if live is not None:
                k = jnp.where(live, k, 0.0)
                lg = jnp.where(live, lg, 0.0)
            items.append((q_ref[rows, ks], k, v_ref[rows, head * GLA_DV:(head + 1) * GLA_DV], lg))
    intra = _gla_intra(items, tri_ref[...], sel_ref[...])

    states = [st_ref[head] for head in range(GLA_HEADS)] if chained else None
    for j in range(n_chunks):
        rows = slice(j * GLA_CHUNK, (j + 1) * GLA_CHUNK)
        for head in range(GLA_HEADS):
            qe, o_intra, decay, upd = intra[j * GLA_HEADS + head]
            st = states[head] if chained else s_in_ref[j, head].T
            o_ref[rows, head * GLA_DV:(head + 1) * GLA_DV] = o_intra + lax.dot_general(
                qe, st.astype(BF16), (((1,), (1,)), ((), ())), preferred_element_type=F32)
            st = st * decay + upd
            if chained:
                states[head] = st
            else:
                s_out_ref[j, head] = st.T

    if chained:
        for head in range(GLA_HEADS):
            st_ref[head] = states[head]

        @pl.when(step == pl.num_programs(0) - 1)
        def _():
            for head in range(GLA_HEADS):
                s_out_ref[0, head] = states[head].T


def _gla(p32, lg, state_in, tri, sel, o_all, *, first_chunk, n_seq_chunks, chained, n_valid=GLA_CHUNK):
    c = GLA_CHUNK
    per_step = _pick_tile(n_seq_chunks, (4, 2, 1))
    assert first_chunk % per_step == 0
    steps = n_seq_chunks // per_step
    row0 = first_chunk // per_step
    q_blk = 3 * SB_WIDTH // GLA_KEY_WIDTH
    v_blk = (3 * SB_WIDTH + 2 * GLA_KEY_WIDTH) // GLA_WIDTH
    n_state = 1 if chained else n_seq_chunks
    state_spec = pl.BlockSpec((1 if chained else per_step, GLA_HEADS, GLA_DK, GLA_DV),
                              (lambda s: (0, 0, 0, 0)) if chained else (lambda s: (s, 0, 0, 0)))
    rows = per_step * c
    args = [p32, p32, p32, lg, state_in, tri, sel]
    if isinstance(o_all, int):
        out_rows, out0, alias_spec, extra = o_all, 0, [], {}
    else:
        out_rows, out0 = o_all.shape[0], row0
        alias_spec, extra = [pl.BlockSpec(memory_space=pl.ANY)], dict(input_output_aliases={len(args): 0})
        args.append(o_all)
    return pl.pallas_call(
        functools.partial(_gla_kernel, n_chunks=per_step, chained=chained, n_valid=n_valid,
                          aliased=bool(alias_spec)),
        grid=(steps,),
        in_specs=[
            pl.BlockSpec((rows, GLA_KEY_WIDTH), lambda s: (row0 + s, q_blk)),
            pl.BlockSpec((rows, GLA_KEY_WIDTH), lambda s: (row0 + s, q_blk + 1)),
            pl.BlockSpec((rows, GLA_WIDTH), lambda s: (row0 + s, v_blk)),
            pl.BlockSpec((rows, GLA_KEY_WIDTH), lambda s: (row0 + s, 0)),
            state_spec,
            pl.BlockSpec(tri.shape, lambda s: (0, 0)),
            pl.BlockSpec(sel.shape, lambda s: (0, 0)),
        ] + alias_spec,
        out_specs=[pl.BlockSpec((rows, GLA_WIDTH), lambda s: (out0 + s, 0)), state_spec],
        out_shape=[jax.ShapeDtypeStruct((out_rows, GLA_WIDTH), F32),
                   jax.ShapeDtypeStruct((n_state, GLA_HEADS, GLA_DK, GLA_DV), F32)],
        scratch_shapes=[pltpu.VMEM((GLA_HEADS, GLA_DV, GLA_DK), F32)],
        compiler_params=_cparams(("arbitrary",)),
        **extra,
    )(*args)


def _mix_kernel(osb_ref, og_ref, gg_ref, x_ref, gnorm_ref, wout_ref, nffn_ref, wr_ref, br_ref, ltri_ref,
                h_ref, xn_ref, idx_ref, gate_ref, rank_ref, cnt_ref, carry_ref):
    step = pl.program_id(0)

    @pl.when(step == 0)
    def _():
        carry_ref[...] = jnp.zeros_like(carry_ref)

    parts = []
    for head in range(GLA_HEADS):
        vs = slice(head * GLA_DV, (head + 1) * GLA_DV)
        og = og_ref[:, vs]
        inv = lax.rsqrt(jnp.mean(og * og, axis=-1, keepdims=True) + RMS_EPS)
        g = gg_ref[:, vs]
        parts.append((((og * inv) * gnorm_ref[...]) * (g * jax.nn.sigmoid(g))).astype(BF16))
    mixed = jnp.dot(jnp.concatenate([osb_ref[...]] + parts, axis=1), wout_ref[...],
                    preferred_element_type=F32)
    h = x_ref[...] + mixed
    h_ref[...] = h

    inv = lax.rsqrt(jnp.mean(h * h, axis=-1, keepdims=True) + RMS_EPS)
    xn = (h * inv) * nffn_ref[...]
    xn_ref[...] = xn

    x_hi = xn.astype(BF16)
    x_lo = (xn - x_hi.astype(F32)).astype(BF16)
    both = jnp.dot(x_hi, wr_ref[...], preferred_element_type=F32)
    logits = ((both[:, :LANES] + both[:, LANES:])
              + jnp.dot(x_lo, wr_ref[:, :LANES], preferred_element_type=F32)) + br_ref[...]
    tm = logits.shape[0]
    lane = lax.broadcasted_iota(jnp.int32, (tm, LANES), 1)
    vals, sels = [], []
    idx_out = jnp.zeros((tm, LANES), jnp.int32)
    for k in range(TOP_K):
        m = jnp.max(logits, axis=-1, keepdims=True)
        idx = jnp.min(jnp.where(logits == m, lane, LANES), axis=-1, keepdims=True)
        sel = lane == idx
        vals.append(m)
        sels.append(sel)
        idx_out = jnp.where(lane == k, idx, idx_out)
        logits = jnp.where(sel, -jnp.inf, logits)
    exps = [jnp.exp(v - vals[0]) for v in vals]
    denom = exps[0] + exps[1] + exps[2] + exps[3]
    gate_out = jnp.zeros((tm, LANES), F32)
    for k in range(TOP_K):
        gate_out = jnp.where(lane == k, exps[k] / denom, gate_out)
    idx_ref[...] = idx_out
    gate_ref[...] = gate_out

    hot = jnp.zeros((tm, LANES), F32)
    for sel in sels:
        hot = jnp.where(sel, 1.0, hot)
    before = jnp.dot(ltri_ref[...], hot.astype(BF16), preferred_element_type=F32) + carry_ref[...]
    rank_out = jnp.zeros((tm, LANES), jnp.int32)
    for k in range(TOP_K):
        rk = jnp.sum(jnp.where(sels[k], before, 0.0), axis=-1, keepdims=True)
        rank_out = jnp.where(lane == k, rk.astype(jnp.int32), rank_out)
    rank_ref[...] = rank_out
    carry_ref[...] = carry_ref[...] + jnp.sum(hot, axis=0, keepdims=True)
    cnt_ref[...] = carry_ref[...]


def _mix(o_sb, o_g, p32, x_all, gnorm, w_out, nffn, w_r, b_r, n_tok):
    tm = _pick_tile(n_tok, (512, 256, 128))
    g_blk = (3 * SB_WIDTH + 2 * GLA_KEY_WIDTH + GLA_WIDTH) // GLA_WIDTH
    ltri = jnp.asarray(np.tril(np.ones((tm, tm), np.float32), -1), BF16)
    const = lambda shape: pl.BlockSpec(shape, lambda i: (0, 0))
    tile = lambda w: pl.BlockSpec((tm, w), lambda i: (i, 0))
    return pl.pallas_call(
        _mix_kernel,
        grid=(n_tok // tm,),
        in_specs=[tile(SB_WIDTH), tile(GLA_WIDTH),
                  pl.BlockSpec((tm, GLA_WIDTH), lambda i: (i, g_blk)),
                  tile(D_MODEL), const((1, GLA_DV)), const((D_MODEL, D_MODEL)), const((1, D_MODEL)),
                  const((D_MODEL, 2 * LANES)), const((1, LANES)), const((tm, tm))],
        out_specs=[tile(D_MODEL), tile(D_MODEL), tile(LANES), tile(LANES), tile(LANES),
                   const((1, LANES))],
        out_shape=[jax.ShapeDtypeStruct((n_tok, D_MODEL), F32),
                   jax.ShapeDtypeStruct((n_tok, D_MODEL), F32),
                   jax.ShapeDtypeStruct((n_tok, LANES), jnp.int32),
                   jax.ShapeDtypeStruct((n_tok, LANES), F32),
                   jax.ShapeDtypeStruct((n_tok, LANES), jnp.int32),
                   jax.ShapeDtypeStruct((1, LANES), F32)],
        scratch_shapes=[pltpu.VMEM((1, LANES), F32)],
        compiler_params=_cparams(("arbitrary",)),
    )(o_sb, o_g, p32, x_all, gnorm, w_out, nffn, w_r, b_r, ltri)


DMA_ISSUE_UNROLL = 8


def _dispatch_kernel(dest_ref, x_ref, xs_ref, sem, *, rows):
    def start(r, carry):
        for k in range(TOP_K):
            pltpu.make_async_copy(x_ref.at[pl.ds(r, 1), :],
                                  xs_ref.at[pl.ds(dest_ref[0, 0, r * TOP_K + k], 1), :], sem).start()
        return carry

    lax.fori_loop(0, rows, start, 0, unroll=DMA_ISSUE_UNROLL)
    for k in range(TOP_K):
        pltpu.make_async_copy(x_ref, xs_ref.at[pl.ds(0, rows), :], sem).wait()


def _dispatch(xp, dest, n_slots):
    n_tok, width = xp.shape
    rows = _pick_tile(n_tok, (256, 128))
    dest3 = dest.reshape(n_tok // rows, 1, rows * TOP_K)
    return pl.pallas_call(
        functools.partial(_dispatch_kernel, rows=rows),
        grid=(n_tok // rows,),
        in_specs=[pl.BlockSpec((1, 1, rows * TOP_K), lambda i: (i, 0, 0), memory_space=pltpu.SMEM),
                  pl.BlockSpec((rows, width), lambda i: (i, 0))],
        out_specs=pl.BlockSpec(memory_space=pl.ANY),
        out_shape=jax.ShapeDtypeStruct((n_slots, width), xp.dtype),
        scratch_shapes=[pltpu.SemaphoreType.DMA(())],
        compiler_params=_cparams(("arbitrary",)),
    )(dest3, xp)


def _expert_changed(be_ref, m):
    prev = be_ref[jnp.maximum(m - 1, 0)]
    return (m == 0) | (be_ref[m] != prev)


def _moe_up_kernel(be_ref, nu_ref, xs_ref, wg_ref, wu_ref, bg_ref, bu_ref, h_ref, w16_ref, *, tf):
    m = pl.program_id(1)

    @pl.when(_expert_changed(be_ref, m))
    def _():
        w16_ref[:, :tf] = wg_ref[...].astype(BF16)
        w16_ref[:, tf:] = wu_ref[...].astype(BF16)

    @pl.when(m < nu_ref[0])
    def _():
        x = xs_ref[...].astype(BF16)
        gu = jnp.dot(x, w16_ref[...], preferred_element_type=F32)
        g = jnp.minimum(gu[:, :tf] + bg_ref[...], SWIGLU_LIMIT)
        u = jnp.clip(gu[:, tf:] + bu_ref[...], -SWIGLU_LIMIT, SWIGLU_LIMIT)
        h_ref[...] = ((u + 1.0) * (g * jax.nn.sigmoid(g * SWIGLU_ALPHA))).astype(h_ref.dtype)


def _moe_down_kernel(be_ref, nu_ref, h_ref, wd_ref, bd_ref, y_ref, w16_ref):
    m = pl.program_id(1)

    @pl.when(_expert_changed(be_ref, m))
    def _():
        w16_ref[...] = wd_ref[...].astype(BF16)

    @pl.when(m < nu_ref[0])
    def _():
        y_ref[...] = jnp.dot(h_ref[...], w16_ref[...], preferred_element_type=F32) + bd_ref[...]


def _moe_experts(xs, block_expert, n_used, w_gate_up, b_gate_up, w_down, b_down):
    n_slots = xs.shape[0]
    n_blocks = n_slots // MOE_TM
    tf = 1024
    n_f = D_FF // tf
    live = lambda m, nu: jnp.minimum(m, nu[0] - 1)
    h = pl.pallas_call(
        functools.partial(_moe_up_kernel, tf=tf),
        grid_spec=pltpu.PrefetchScalarGridSpec(
            num_scalar_prefetch=2,
            grid=(n_f, n_blocks),
            in_specs=[
                pl.BlockSpec((MOE_TM, D_MODEL), lambda j, m, be, nu: (live(m, nu), 0)),
                pl.BlockSpec((None, D_MODEL, tf), lambda j, m, be, nu: (be[m], 0, j)),
                pl.BlockSpec((None, D_MODEL, tf), lambda j, m, be, nu: (be[m], 0, n_f + j)),
                pl.BlockSpec((None, 1, tf), lambda j, m, be, nu: (be[m], 0, j)),
                pl.BlockSpec((None, 1, tf), lambda j, m, be, nu: (be[m], 0, n_f + j)),
            ],
            out_specs=pl.BlockSpec((MOE_TM, tf), lambda j, m, be, nu: (live(m, nu), j)),
            scratch_shapes=[pltpu.VMEM((D_MODEL, 2 * tf), BF16)],
        ),
        out_shape=jax.ShapeDtypeStruct((n_slots, D_FF), BF16),
        compiler_params=_cparams(("arbitrary", "arbitrary")),
    )(block_expert, n_used, xs, w_gate_up, w_gate_up, b_gate_up, b_gate_up)

    return pl.pallas_call(
        _moe_down_kernel,
        grid_spec=pltpu.PrefetchScalarGridSpec(
            num_scalar_prefetch=2,
            grid=(1, n_blocks),
            in_specs=[
                pl.BlockSpec((MOE_TM, D_FF), lambda j, m, be, nu: (live(m, nu), 0)),
                pl.BlockSpec((None, D_FF, D_MODEL), lambda j, m, be, nu: (be[m], 0, 0)),
                pl.BlockSpec((None, 1, D_MODEL), lambda j, m, be, nu: (be[m], 0, 0)),
            ],
            out_specs=pl.BlockSpec((MOE_TM, D_MODEL), lambda j, m, be, nu: (live(m, nu), 0)),
            scratch_shapes=[pltpu.VMEM((D_FF, D_MODEL), BF16)],
        ),
        out_shape=jax.ShapeDtypeStruct((n_slots, D_MODEL), F32),
        compiler_params=_cparams(("arbitrary", "arbitrary")),
    )(block_expert, n_used, h, w_down, b_down)


def _combine_kernel(dest_ref, y_ref, gate_ref, h_ref, nf_ref, outp_ref, outs_ref, buf_ref, sem, *,
                    rows, prompt_tiles):
    def start(r, carry):
        for k in range(TOP_K):
            pltpu.make_async_copy(y_ref.at[pl.ds(dest_ref[0, 0, r * TOP_K + k], 1), :],
                                  buf_ref.at[k, pl.ds(r, 1), :], sem).start()
        return carry

    lax.fori_loop(0, rows, start, 0, unroll=DMA_ISSUE_UNROLL)
    for k in range(TOP_K):
        pltpu.make_async_copy(y_ref.at[pl.ds(0, rows), :], buf_ref.at[k], sem).wait()

    gate = gate_ref[...]
    moe = buf_ref[0] * gate[:, 0:1]
    for k in range(1, TOP_K):
        moe = moe + buf_ref[k] * gate[:, k:k + 1]
    h = h_ref[...] + moe
    inv = lax.rsqrt(jnp.mean(h * h, axis=-1, keepdims=True) + RMS_EPS)
    y = (h * inv) * nf_ref[...]
    is_prompt = pl.program_id(0) < prompt_tiles

    @pl.when(is_prompt)
    def _():
        outp_ref[...] = y

    @pl.when(jnp.logical_not(is_prompt))
    def _():
        outs_ref[...] = y


def _combine(y_slots, dest, gate, h, norm_final, n_prompt):
    n_tok = h.shape[0]
    rows = _pick_tile(math.gcd(n_prompt, n_tok - n_prompt), (256, 128))
    prompt_tiles = n_prompt // rows
    dest3 = dest.reshape(n_tok // rows, 1, rows * TOP_K)
    return pl.pallas_call(
        functools.partial(_combine_kernel, rows=rows, prompt_tiles=prompt_tiles),
        grid=(n_tok // rows,),
        in_specs=[pl.BlockSpec((1, 1, rows * TOP_K), lambda i: (i, 0, 0), memory_space=pltpu.SMEM),
                  pl.BlockSpec(memory_space=pl.ANY),
                  pl.BlockSpec((rows, LANES), lambda i: (i, 0)),
                  pl.BlockSpec((rows, D_MODEL), lambda i: (i, 0)),
                  pl.BlockSpec((1, D_MODEL), lambda i: (0, 0))],
        out_specs=[pl.BlockSpec((rows, D_MODEL), lambda i: (jnp.minimum(i, prompt_tiles - 1), 0)),
                   pl.BlockSpec((rows, D_MODEL), lambda i: (jnp.maximum(i - prompt_tiles, 0), 0))],
        out_shape=[jax.ShapeDtypeStruct((n_prompt, D_MODEL), F32),
                   jax.ShapeDtypeStruct((n_tok - n_prompt, D_MODEL), F32)],
        scratch_shapes=[pltpu.VMEM((TOP_K, rows, D_MODEL), F32), pltpu.SemaphoreType.DMA(())],
        compiler_params=_cparams(("arbitrary",)),
    )(dest3, y_slots, gate, h, norm_final)


def kernel(x_prompt, x_sample, cache_sb_k, cache_sb_v, state_gla, meta_tokens, norm_mix, w_in, w_gk2, b_gk,
           gla_norm, w_out, norm_ffn, w_router, b_router, w_gate_up, b_gate_up, w_down, b_down, norm_final):
    assert x_prompt.shape[0] == 1 and norm_mix.shape[0] == 1, "one prompt stream, one layer"
    n_prompt = x_prompt.shape[1]
    n_batch, n_new = x_sample.shape[:2]
    n_past = cache_sb_k.shape[2]
    assert n_new == GLA_CHUNK and n_prompt % SB_TQ == 0 and n_past % (SB_TK * SB_CACHE_SPLIT) == 0
    n_tok = n_prompt + n_batch * n_new
    assert n_tok % META_BLOCK == 0

    x_all = jnp.concatenate([
        x_prompt[0], x_sample.reshape(n_batch * n_new, D_MODEL), meta_tokens.astype(F32),
        jnp.zeros((META_BLOCK - N_META, D_MODEL), F32)], axis=0)

    w_in0 = w_in[0]
    w_main = w_in0[:, :D_MAIN].astype(BF16)
    w_a = jnp.pad(w_in0[:, D_MAIN:], ((0, 0), (0, LANES - GLA_RANK))).astype(BF16)
    w_gk = jnp.pad(w_gk2[0], ((0, LANES - GLA_RANK), (0, 0)))
    colscale = np.ones((1, D_MAIN), np.float32)
    colscale[0, :SB_WIDTH] = SB_DIM ** -0.5 * LOG2E
    colscale[0, 3 * SB_WIDTH:3 * SB_WIDTH + GLA_KEY_WIDTH] = GLA_DK ** -0.5
    p32, p16, lg = _inproj(x_all, norm_mix, w_main, jnp.asarray(colscale), w_a, w_gk, b_gk)

    u = jnp.asarray(np.tril(np.ones((SB_TK, SB_TK), np.float32)), BF16)
    o_sb = _sb_prompt(p16, u, n_prompt, n_tok)
    o_sb = _sb_sample(p16, cache_sb_k[0].reshape(n_batch, n_past * SB_HEADS, SB_DIM),
                      cache_sb_v[0].reshape(n_batch, n_past * SB_HEADS, SB_DIM), u, o_sb, n_prompt, n_batch, n_new)

    tri, sel = _gla_constants()
    c = GLA_CHUNK
    zero_state = jnp.zeros((1, GLA_HEADS, GLA_DK, GLA_DV), F32)
    _, state_meta = _gla(p32, lg, zero_state, tri, sel, c, first_chunk=n_tok // c, n_seq_chunks=1,
                         chained=True, n_valid=N_META)
    o_g, state_p = _gla(p32, lg, state_meta, tri, sel, n_tok, first_chunk=0, n_seq_chunks=n_prompt // c,
                        chained=True)
    o_g, state_s = _gla(p32, lg, state_gla[0].astype(F32), tri, sel, o_g, first_chunk=n_prompt // c,
                        n_seq_chunks=n_batch, chained=False)

    w_r = jnp.pad(w_router[0].astype(F32), ((0, 0), (0, LANES - N_EXPERTS)))
    w_r_hi = w_r.astype(BF16)
    w_r = jnp.concatenate([w_r_hi, (w_r - w_r_hi.astype(F32)).astype(BF16)], axis=1)
    b_r = jnp.pad(b_router.astype(F32), ((0, 0), (0, LANES - N_EXPERTS)), constant_values=-jnp.inf)
    h, xp, top_idx, gate, rank, counts = _mix(o_sb, o_g, p32, x_all, gla_norm, w_out[0].astype(BF16),
                                              norm_ffn, w_r, b_r, n_tok)

    counts = counts[0, :N_EXPERTS].astype(jnp.int32)
    padded = (counts + MOE_TM - 1) // MOE_TM * MOE_TM
    ends = jnp.cumsum(padded)
    n_blocks = -(-n_tok * TOP_K // MOE_TM) + N_EXPERTS
    dest = ((ends - padded)[top_idx[:, :TOP_K]] + rank[:, :TOP_K]).astype(jnp.int32)
    block_start = jnp.arange(n_blocks, dtype=jnp.int32) * MOE_TM
    block_expert = jnp.minimum(jnp.sum((ends[None, :] <= block_start[:, None]).astype(jnp.int32), axis=1),
                               N_EXPERTS - 1)
    n_used = (ends[-1:] // MOE_TM).astype(jnp.int32)

    xs = _dispatch(xp, dest, n_blocks * MOE_TM)
    y_slots = _moe_experts(xs, block_expert, n_used, w_gate_up[0], b_gate_up[0][:, None, :],
                           w_down[0], b_down[0][:, None, :])
    y_prompt, y_sample = _combine(y_slots, dest, gate, h, norm_final[None, :], n_prompt)
    y_prompt = y_prompt[None]
    y_sample = y_sample.reshape(n_batch, n_new, D_MODEL)
    k_lo, v_lo = SB_WIDTH, 2 * SB_WIDTH

    def kv_prompt(lo):
        rows = jnp.concatenate([p32[n_tok:n_tok + N_META, lo:lo + SB_WIDTH], p32[:n_prompt, lo:lo + SB_WIDTH]])
        return rows.reshape(1, 1, N_META + n_prompt, SB_HEADS, SB_DIM)

    def kv_sample(lo):
        return p32[n_prompt:n_tok, lo:lo + SB_WIDTH].reshape(1, n_batch, n_new, SB_HEADS, SB_DIM)

    return (y_prompt, y_sample, kv_prompt(k_lo), kv_prompt(v_lo), state_p[None],
            kv_sample(k_lo), kv_sample(v_lo), state_s[None])
```

```python
import functools
import math

import jax
import jax.numpy as jnp
import numpy as np
from jax import lax
from jax.experimental import pallas as pl
from jax.experimental.pallas import tpu as pltpu

F32 = jnp.float32
BF16 = jnp.bfloat16
HIGHEST = lax.Precision.HIGHEST

D_MODEL = 2048
N_META = 16
META_BLOCK = 128
SB_HEADS = 8
SB_DIM = 128
SB_WIDTH = SB_HEADS * SB_DIM
GLA_HEADS = 4
GLA_DK = 128
GLA_DV = 256
GLA_KEY_WIDTH = GLA_HEADS * GLA_DK
GLA_WIDTH = GLA_HEADS * GLA_DV
GLA_RANK = 16
GLA_TAU = 16.0
GLA_CHUNK = 64
N_EXPERTS = 32
TOP_K = 4
D_FF = 2048
SWIGLU_LIMIT = 7.0
SWIGLU_ALPHA = 1.702
RMS_EPS = 1e-5
D_MAIN = 3 * SB_WIDTH + 2 * GLA_KEY_WIDTH + 2 * GLA_WIDTH
LANES = 128
LOG2E = 1.4426950408889634

SB_TQ = 512
SB_TK = 256
SB_STEPS_PER_TRIP = 4
MOE_TM = 512
VMEM_LIMIT = 58 * 1024 * 1024


def _cparams(sem):
    return pltpu.CompilerParams(dimension_semantics=sem, vmem_limit_bytes=VMEM_LIMIT)


def _pick_tile(n, cands):
    for c in cands:
        if n % c == 0:
            return c
    raise ValueError(f"no tile for {n} in {cands}")


def _softplus(z):
    return jnp.maximum(z, 0.0) + jnp.log(1.0 + jnp.exp(-jnp.abs(z)))


def _inproj_kernel(x_ref, gain_ref, w_ref, scale_ref, wa_ref, wgk_ref, bgk_ref,
                   p32_ref, p16_ref, lg_ref, xn_ref):
    @pl.when(pl.program_id(1) == 0)
    def _():
        x = x_ref[...]
        inv = lax.rsqrt(jnp.mean(x * x, axis=-1, keepdims=True) + RMS_EPS)
        xn = ((x * inv) * gain_ref[...]).astype(BF16)
        xn_ref[...] = xn
        a = jnp.dot(xn, wa_ref[...], preferred_element_type=F32)
        pre = jnp.dot(a, wgk_ref[...], preferred_element_type=F32, precision=HIGHEST) + bgk_ref[...]
        lg_ref[...] = -_softplus(-pre) * (1.0 / GLA_TAU)

    y = jnp.dot(xn_ref[...], w_ref[...], preferred_element_type=F32) * scale_ref[...]
    p32_ref[...] = y
    p16_ref[...] = y.astype(BF16)


def _inproj(x_all, gain, w_main, colscale, w_a, w_gk, b_gk):
    rows = x_all.shape[0]
    tm = _pick_tile(rows, (640, 512, 256, 128))
    tn = 1024
    return pl.pallas_call(
        _inproj_kernel,
        grid=(rows // tm, D_MAIN // tn),
        in_specs=[
            pl.BlockSpec((tm, D_MODEL), lambda i, j: (i, 0)),
            pl.BlockSpec((1, D_MODEL), lambda i, j: (0, 0)),
            pl.BlockSpec((D_MODEL, tn), lambda i, j: (0, j)),
            pl.BlockSpec((1, tn), lambda i, j: (0, j)),
            pl.BlockSpec((D_MODEL, LANES), lambda i, j: (0, 0)),
            pl.BlockSpec((LANES, GLA_KEY_WIDTH), lambda i, j: (0, 0)),
            pl.BlockSpec((1, GLA_KEY_WIDTH), lambda i, j: (0, 0)),
        ],
        out_specs=[
            pl.BlockSpec((tm, tn), lambda i, j: (i, j)),
            pl.BlockSpec((tm, tn), lambda i, j: (i, j)),
            pl.BlockSpec((tm, GLA_KEY_WIDTH), lambda i, j: (i, 0)),
        ],
        out_shape=[
            jax.ShapeDtypeStruct((rows, D_MAIN), F32),
            jax.ShapeDtypeStruct((rows, D_MAIN), BF16),
            jax.ShapeDtypeStruct((rows, GLA_KEY_WIDTH), F32),
        ],
        scratch_shapes=[pltpu.VMEM((tm, D_MODEL), BF16)],
        compiler_params=_cparams(("arbitrary", "arbitrary")),
    )(x_all, gain, w_main, colscale, w_a, w_gk, b_gk)


def _sb_softplus2(z, allowed):
    sp = jnp.maximum(z, 0.0) + jnp.log2(1.0 + jnp.exp2(-jnp.abs(z)))
    if allowed is not None:
        sp = jnp.where(allowed, sp, 0.0)
    return sp


def _sb_mass(z, u, allowed):
    return jnp.dot(_sb_softplus2(z, allowed).astype(BF16), u, preferred_element_type=F32)


def _sb_weight(z, r, c, allowed):
    w = jnp.exp2(z - r - c)
    if allowed is not None:
        w = jnp.where(allowed, w, 0.0)
    return w.astype(BF16)


def _sb_visit_many(qs, kblks, vblks, u, cs, alloweds):
    tk = u.shape[0]
    n = len(alloweds)
    zs = [lax.dot_general(q, kblk, (((1,), (1,)), ((), ())), preferred_element_type=F32)
          for q, kblk in zip(qs, kblks)]
    rs = [[_sb_mass(z[:, s * tk:(s + 1) * tk], u, alloweds[s]) for s in range(n)] for z in zs]
    ws, c_out = [], []
    for z, r, c in zip(zs, rs, cs):
        parts = [None] * n
        for s in reversed(range(n)):
            parts[s] = _sb_weight(z[:, s * tk:(s + 1) * tk], r[s], c, alloweds[s])
            c = c + r[s][:, 0:1]
        ws.append(parts[0] if n == 1 else jnp.concatenate(parts, axis=1))
        c_out.append(c)
    pvs = [jnp.dot(w, vblk, preferred_element_type=F32) for w, vblk in zip(ws, vblks)]
    return list(zip(pvs, c_out))


def _sb_visit(q, kblk, vblk, u, c, alloweds):
    return _sb_visit_many([q], [kblk], [vblk], u, [c], alloweds)[0]


def _sb_prompt_kernel(q_ref, k_ref, v_ref, km_ref, vm_ref, u_ref, o_ref,
                      acc_ref, c_ref, z_ref, sp_ref, t_ref, tot_ref):
    i = pl.program_id(1)
    q = q_ref[pl.ds(pl.multiple_of(i * SB_TQ, SB_TQ), SB_TQ), :]
    u = u_ref[...]
    n_sub = SB_TQ // SB_TK

    def key_rows(blk):
        return pl.ds(pl.multiple_of(blk * SB_TQ, SB_TQ), SB_TQ)

    def score_matmul(blk):
        return lax.dot_general(q, k_ref[key_rows(blk), :], (((1,), (1,)), ((), ())),
                               preferred_element_type=F32)

    def stage_softplus(z, slot, alloweds):
        z_ref[slot] = z
        for s in range(n_sub):
            cols = slice(s * SB_TK, (s + 1) * SB_TK)
            sp_ref[slot, :, cols] = _sb_softplus2(z[:, cols], alloweds[s]).astype(BF16)

    def stage_mass(slot, alloweds):
        for s in range(n_sub):
            cols = slice(s * SB_TK, (s + 1) * SB_TK)
            r_s = jnp.dot(sp_ref[slot, :, cols], u, preferred_element_type=F32)
            t_s = z_ref[slot, :, cols] - r_s
            if alloweds[s] is not None:
                t_s = jnp.where(alloweds[s], t_s, -jnp.inf)
            t_ref[:, cols] = t_s
            tot_ref[s] = r_s[:, 0:1]

    def stage_accumulate(blk):
        c = c_ref[...]
        ws = [None] * n_sub
        for s in reversed(range(n_sub)):
            ws[s] = jnp.exp2(t_ref[:, s * SB_TK:(s + 1) * SB_TK] - c).astype(BF16)
            c = c + tot_ref[s]
        acc_ref[...] += jnp.dot(jnp.concatenate(ws, axis=1), v_ref[key_rows(blk), :],
                                preferred_element_type=F32)
        c_ref[...] = c

    acc_ref[...] = jnp.zeros_like(acc_ref)
    c_ref[...] = jnp.zeros_like(c_ref)
    row_i = lax.broadcasted_iota(jnp.int32, (SB_TQ, SB_TK), 0)
    col_i = lax.broadcasted_iota(jnp.int32, (SB_TQ, SB_TK), 1)
    own = [col_i + s * SB_TK < row_i for s in range(n_sub)]
    free = [None] * n_sub
    stage_softplus(score_matmul(i), 0, own)
    stage_mass(0, own)
    stage_softplus(score_matmul(jnp.maximum(i - 1, 0)), 1, free)

    def step(p, slot):
        stage_softplus(score_matmul(i - p - 2), slot, free)
        stage_accumulate(i - p)
        stage_mass(1 - slot, free)

    def body(n, carry):
        for k in range(SB_STEPS_PER_TRIP):
            step(SB_STEPS_PER_TRIP * n + k, k % 2)
        return carry

    n_full = jnp.maximum(i - 1, 0)
    lax.fori_loop(0, n_full // SB_STEPS_PER_TRIP, body, 0)
    done = n_full - n_full % SB_STEPS_PER_TRIP
    for k in range(SB_STEPS_PER_TRIP - 1):
        @pl.when(n_full - done > k)
        def _():
            step(done + k, k % 2)

    @pl.when(i >= 1)
    def _():
        stage_accumulate(1)
        stage_mass(i % 2, free)

    stage_accumulate(0)

    meta_cols = lax.broadcasted_iota(jnp.int32, (SB_TQ, META_BLOCK), 1)
    pv, _ = _sb_visit(q, km_ref[...], vm_ref[...], u[:META_BLOCK, :META_BLOCK], c_ref[...],
                      [meta_cols < N_META])
    o_ref[...] = (acc_ref[...] + pv).astype(o_ref.dtype)


def _sb_prompt(p16, u, n_prompt, n_tok):
    head_spec = lambda off: pl.BlockSpec((n_prompt, SB_DIM), lambda h, i: (0, off + h))
    meta_spec = lambda off: pl.BlockSpec((META_BLOCK, SB_DIM), lambda h, i: (n_tok // META_BLOCK, off + h))
    return pl.pallas_call(
        _sb_prompt_kernel,
        grid=(SB_HEADS, n_prompt // SB_TQ),
        in_specs=[head_spec(0), head_spec(SB_HEADS), head_spec(2 * SB_HEADS),
                  meta_spec(SB_HEADS), meta_spec(2 * SB_HEADS),
                  pl.BlockSpec((SB_TK, SB_TK), lambda h, i: (0, 0))],
        out_specs=pl.BlockSpec((SB_TQ, SB_DIM), lambda h, i: (i, h)),
        out_shape=jax.ShapeDtypeStruct((n_tok, SB_WIDTH), BF16),
        scratch_shapes=[pltpu.VMEM((SB_TQ, SB_DIM), F32), pltpu.VMEM((SB_TQ, 1), F32),
                        pltpu.VMEM((2, SB_TQ, SB_TQ), F32), pltpu.VMEM((2, SB_TQ, SB_TQ), BF16),
                        pltpu.VMEM((SB_TQ, SB_TQ), F32), pltpu.VMEM((SB_TQ // SB_TK, SB_TQ, 1), F32)],
        compiler_params=_cparams(("arbitrary", "arbitrary")),
    )(p16, p16, p16, p16, p16, u)


SB_SAMPLE_GROUP = 8
SB_CACHE_SPLIT = 2


def _sb_sample_kernel(q_ref, kn_ref, vn_ref, kc_ref, vc_ref, u_ref, o_prev_ref, o_ref, acc_ref, c_ref, *,
                      n_new, n_pos):
    first = pl.program_id(1) == 0

    @pl.when(first & (pl.program_id(0) == 0))
    def _():
        acc_ref[...] = jnp.zeros_like(acc_ref)
        c_ref[...] = jnp.zeros_like(c_ref)

    u = u_ref[...]
    r_i = lax.broadcasted_iota(jnp.int32, (n_new, n_new), 0)
    c_i = lax.broadcasted_iota(jnp.int32, (n_new, n_new), 1)
    cols = [slice(head * SB_DIM, (head + 1) * SB_DIM) for head in range(SB_HEADS)]
    for group in range(0, SB_HEADS, SB_SAMPLE_GROUP):
        heads = range(group, group + SB_SAMPLE_GROUP)
        qs = [q_ref[:, cols[head]] for head in heads]
        new = _sb_visit_many(qs, [kn_ref[:, cols[head]] for head in heads],
                             [vn_ref[:, cols[head]] for head in heads], u[:n_new, :n_new],
                             [jnp.zeros((n_new, 1), F32)] * SB_SAMPLE_GROUP, [c_i < r_i])
        accs = [jnp.where(first, pv, acc_ref[head]) for head, (pv, _) in zip(heads, new)]
        cs = [jnp.where(first, tot, c_ref[head]) for head, (_, tot) in zip(heads, new)]
        kblks = [kc_ref[pl.ds(head, n_pos, stride=SB_HEADS), :].astype(BF16) for head in heads]
        vblks = [vc_ref[pl.ds(head, n_pos, stride=SB_HEADS), :].astype(BF16) for head in heads]
        old = _sb_visit_many(qs, kblks, vblks, u, cs, [None] * (n_pos // SB_TK))
        for head, acc0, (pv, c) in zip(heads, accs, old):
            acc = acc0 + pv
            acc_ref[head] = acc
            c_ref[head] = c
            o_ref[:, cols[head]] = acc.astype(o_ref.dtype)


def _sb_sample(p16, cache_k, cache_v, u, o_all, n_prompt, n_batch, n_new):
    n_past = cache_k.shape[1] // SB_HEADS
    n_pos = n_past // SB_CACHE_SPLIT
    assert n_pos % SB_TK == 0
    row0 = n_prompt // n_new
    new_spec = lambda blk: pl.BlockSpec((n_new, SB_WIDTH), lambda b, j: (row0 + b, blk))
    cache_spec = pl.BlockSpec((None, n_pos * SB_HEADS, SB_DIM), lambda b, j: (b, SB_CACHE_SPLIT - 1 - j, 0))
    return pl.pallas_call(
        functools.partial(_sb_sample_kernel, n_new=n_new, n_pos=n_pos),
        grid=(n_batch, SB_CACHE_SPLIT),
        in_specs=[new_spec(0), new_spec(1), new_spec(2), cache_spec, cache_spec,
                  pl.BlockSpec((SB_TK, SB_TK), lambda b, j: (0, 0)),
                  pl.BlockSpec(memory_space=pl.ANY)],
        out_specs=pl.BlockSpec((n_new, SB_WIDTH), lambda b, j: (row0 + b, 0)),
        out_shape=jax.ShapeDtypeStruct(o_all.shape, o_all.dtype),
        input_output_aliases={6: 0},
        scratch_shapes=[pltpu.VMEM((SB_HEADS, n_new, SB_DIM), F32), pltpu.VMEM((SB_HEADS, n_new, 1), F32)],
        compiler_params=_cparams(("arbitrary", "arbitrary")),
    )(p16, p16, p16, cache_k, cache_v, u, o_all)


GLA_LEVELS = 6


def _gla_constants():
    c = GLA_CHUNK
    tri = np.tril(np.ones((c, c), np.float32))
    sel = np.zeros((GLA_LEVELS * c, c), np.float32)
    for lvl in range(GLA_LEVELS):
        n = c >> lvl
        for row in range(c):
            sel[lvl * c + row, (row // n) * n + n // 2 - 1] = 1.0
    return jnp.asarray(tri), jnp.asarray(sel, BF16)


def _select_rows_exact(sel, x):
    width = x.shape[1]
    p1 = x.astype(BF16)
    rem = x - p1.astype(F32)
    p2 = rem.astype(BF16)
    p3 = (rem - p2.astype(F32)).astype(BF16)
    out = jnp.dot(sel, jnp.concatenate([p1, p2, p3], axis=1), preferred_element_type=F32)
    return (out[:, :width] + out[:, width:2 * width]) + out[:, 2 * width:]


def _gla_intra(items, tri, sel):
    c = GLA_CHUNK
    n = len(items)
    t_row = lax.broadcasted_iota(jnp.int32, (c, 1), 0)
    t_idx = lax.broadcasted_iota(jnp.int32, (c, c), 0)
    s_idx = lax.broadcasted_iota(jnp.int32, (c, c), 1)
    qk = lambda a, bb: lax.dot_general(a.astype(BF16), bb.astype(BF16), (((1,), (1,)), ((), ())),
                                       preferred_element_type=F32)
    bs = [jnp.dot(tri, lg, preferred_element_type=F32, precision=HIGHEST) for _, _, _, lg in items]
    refs = [_select_rows_exact(sel, b) for b in bs]
    atts = [jnp.where(t_idx == s_idx, qk(q, k), 0.0) for q, k, _, _ in items]
    for lvl in range(GLA_LEVELS):
        shift = GLA_LEVELS - 1 - lvl
        late = ((t_row >> shift) & 1) == 1
        pair = ((t_idx >> (shift + 1)) == (s_idx >> (shift + 1))) \
            & (((t_idx >> shift) & 1) == 1) & (((s_idx >> shift) & 1) == 0)
        parts = []
        for i, (q, k, _, _) in enumerate(items):
            r = refs[i][lvl * c:(lvl + 1) * c, :]
            qa = jnp.where(late, q * jnp.exp(bs[i] - r), 0.0)
            ka = jnp.where(late, 0.0, k * jnp.exp(r - bs[i]))
            parts.append(qk(qa, ka))
        atts = [atts[i] + jnp.where(pair, parts[i], 0.0) for i in range(n)]
    o_intra = [jnp.dot(atts[i].astype(BF16), items[i][2].astype(BF16), preferred_element_type=F32)
               for i in range(n)]
    out = []
    for i, (q, k, v, _) in enumerate(items):
        b_last = bs[i][c - 1:c, :]
        kd = (k * jnp.exp(b_last - bs[i])).astype(BF16)
        upd = lax.dot_general(v.astype(BF16), kd, (((0,), (0,)), ((), ())), preferred_element_type=F32)
        out.append(((q * jnp.exp(bs[i])).astype(BF16), o_intra[i], jnp.exp(b_last), upd))
    return out


def _gla_kernel(q_ref, k_ref, v_ref, lg_ref, s_in_ref, tri_ref, sel_ref, *rest, n_chunks, chained, n_valid,
                aliased):
    o_ref, s_out_ref, st_ref = rest[1:] if aliased else rest
    step = pl.program_id(0)
    live = None
    if n_valid != GLA_CHUNK:
        live = lax.broadcasted_iota(jnp.int32, (GLA_CHUNK, 1), 0) < n_valid

    if chained:
        @pl.when(step == 0)
        def _():
            for head in range(GLA_HEADS):
                st_ref[head] = s_in_ref[0, head].T

    items = []
    for j in range(n_chunks):
        rows = slice(j * GLA_CHUNK, (j + 1) * GLA_CHUNK)
        for head in range(GLA_HEADS):
            ks = slice(head * GLA_DK, (head + 1) * GLA_DK)
            k = k_ref[rows, ks]
            lg = lg_ref[rows, ks]
            if live is not None:
                k = jnp.where(live, k, 0.0)
                lg = jnp.where(live, lg, 0.0)
            items.append((q_ref[rows, ks], k, v_ref[rows, head * GLA_DV:(head + 1) * GLA_DV], lg))
    intra = _gla_intra(items, tri_ref[...], sel_ref[...])

    states = [st_ref[head] for head in range(GLA_HEADS)] if chained else None
    for j in range(n_chunks):
        rows = slice(j * GLA_CHUNK, (j + 1) * GLA_CHUNK)
        for head in range(GLA_HEADS):
            qe, o_intra, decay, upd = intra[j * GLA_HEADS + head]
            st = states[head] if chained else s_in_ref[j, head].T
            o_ref[rows, head * GLA_DV:(head + 1) * GLA_DV] = o_intra + lax.dot_general(
                qe, st.astype(BF16), (((1,), (1,)), ((), ())), preferred_element_type=F32)
            st = st * decay + upd
            if chained:
                states[head] = st
            else:
                s_out_ref[j, head] = st.T

    if chained:
        for head in range(GLA_HEADS):
            st_ref[head] = states[head]

        @pl.when(step == pl.num_programs(0) - 1)
        def _():
            for head in range(GLA_HEADS):
                s_out_ref[0, head] = states[head].T


def _gla(p32, lg, state_in, tri, sel, o_all, *, first_chunk, n_seq_chunks, chained, n_valid=GLA_CHUNK):
    c = GLA_CHUNK
    per_step = _pick_tile(n_seq_chunks, (4, 2, 1))
    assert first_chunk % per_step == 0
    steps = n_seq_chunks // per_step
    row0 = first_chunk // per_step
    q_blk = 3 * SB_WIDTH // GLA_KEY_WIDTH
    v_blk = (3 * SB_WIDTH + 2 * GLA_KEY_WIDTH) // GLA_WIDTH
    n_state = 1 if chained else n_seq_chunks
    state_spec = pl.BlockSpec((1 if chained else per_step, GLA_HEADS, GLA_DK, GLA_DV),
                              (lambda s: (0, 0, 0, 0)) if chained else (lambda s: (s, 0, 0, 0)))
    rows = per_step * c
    args = [p32, p32, p32, lg, state_in, tri, sel]
    if isinstance(o_all, int):
        out_rows, out0, alias_spec, extra = o_all, 0, [], {}
    else:
        out_rows, out0 = o_all.shape[0], row0
        alias_spec, extra = [pl.BlockSpec(memory_space=pl.ANY)], dict(input_output_aliases={len(args): 0})
        args.append(o_all)
    return pl.pallas_call(
        functools.partial(_gla_kernel, n_chunks=per_step, chained=chained, n_valid=n_valid,
                          aliased=bool(alias_spec)),
        grid=(steps,),
        in_specs=[
            pl.BlockSpec((rows, GLA_KEY_WIDTH), lambda s: (row0 + s, q_blk)),
            pl.BlockSpec((rows, GLA_KEY_WIDTH), lambda s: (row0 + s, q_blk + 1)),
            pl.BlockSpec((rows, GLA_WIDTH), lambda s: (row0 + s, v_blk)),
            pl.BlockSpec((rows, GLA_KEY_WIDTH), lambda s: (row0 + s, 0)),
            state_spec,
            pl.BlockSpec(tri.shape, lambda s: (0, 0)),
            pl.BlockSpec(sel.shape, lambda s: (0, 0)),
        ] + alias_spec,
        out_specs=[pl.BlockSpec((rows, GLA_WIDTH), lambda s: (out0 + s, 0)), state_spec],
        out_shape=[jax.ShapeDtypeStruct((out_rows, GLA_WIDTH), F32),
                   jax.ShapeDtypeStruct((n_state, GLA_HEADS, GLA_DK, GLA_DV), F32)],
        scratch_shapes=[pltpu.VMEM((GLA_HEADS, GLA_DV, GLA_DK), F32)],
        compiler_params=_cparams(("arbitrary",)),
        **extra,
    )(*args)


def _mix_kernel(osb_ref, og_ref, gg_ref, x_ref, gnorm_ref, wout_ref, nffn_ref, wr_ref, br_ref, ltri_ref,
                h_ref, xn_ref, idx_ref, gate_ref, rank_ref, cnt_ref, carry_ref):
    step = pl.program_id(0)

    @pl.when(step == 0)
    def _():
        carry_ref[...] = jnp.zeros_like(carry_ref)

    parts = []
    for head in range(GLA_HEADS):
        vs = slice(head * GLA_DV, (head + 1) * GLA_DV)
        og = og_ref[:, vs]
        inv = lax.rsqrt(jnp.mean(og * og, axis=-1, keepdims=True) + RMS_EPS)
        g = gg_ref[:, vs]
        parts.append((((og * inv) * gnorm_ref[...]) * (g * jax.nn.sigmoid(g))).astype(BF16))
    mixed = jnp.dot(jnp.concatenate([osb_ref[...]] + parts, axis=1), wout_ref[...],
                    preferred_element_type=F32)
    h = x_ref[...] + mixed
    h_ref[...] = h

    inv = lax.rsqrt(jnp.mean(h * h, axis=-1, keepdims=True) + RMS_EPS)
    xn = (h * inv) * nffn_ref[...]
    xn_ref[...] = xn

    x_hi = xn.astype(BF16)
    x_lo = (xn - x_hi.astype(F32)).astype(BF16)
    both = jnp.dot(x_hi, wr_ref[...], preferred_element_type=F32)
    logits = ((both[:, :LANES] + both[:, LANES:])
              + jnp.dot(x_lo, wr_ref[:, :LANES], preferred_element_type=F32)) + br_ref[...]
    tm = logits.shape[0]
    lane = lax.broadcasted_iota(jnp.int32, (tm, LANES), 1)
    vals, sels = [], []
    idx_out = jnp.zeros((tm, LANES), jnp.int32)
    for k in range(TOP_K):
        m = jnp.max(logits, axis=-1, keepdims=True)
        idx = jnp.min(jnp.where(logits == m, lane, LANES), axis=-1, keepdims=True)
        sel = lane == idx
        vals.append(m)
        sels.append(sel)
        idx_out = jnp.where(lane == k, idx, idx_out)
        logits = jnp.where(sel, -jnp.inf, logits)
    exps = [jnp.exp(v - vals[0]) for v in vals]
    denom = exps[0] + exps[1] + exps[2] + exps[3]
    gate_out = jnp.zeros((tm, LANES), F32)
    for k in range(TOP_K):
        gate_out = jnp.where(lane == k, exps[k] / denom, gate_out)
    idx_ref[...] = idx_out
    gate_ref[...] = gate_out

    hot = jnp.zeros((tm, LANES), F32)
    for sel in sels:
        hot = jnp.where(sel, 1.0, hot)
    before = jnp.dot(ltri_ref[...], hot.astype(BF16), preferred_element_type=F32) + carry_ref[...]
    rank_out = jnp.zeros((tm, LANES), jnp.int32)
    for k in range(TOP_K):
        rk = jnp.sum(jnp.where(sels[k], before, 0.0), axis=-1, keepdims=True)
        rank_out = jnp.where(lane == k, rk.astype(jnp.int32), rank_out)
    rank_ref[...] = rank_out
    carry_ref[...] = carry_ref[...] + jnp.sum(hot, axis=0, keepdims=True)
    cnt_ref[...] = carry_ref[...]


def _mix(o_sb, o_g, p32, x_all, gnorm, w_out, nffn, w_r, b_r, n_tok):
    tm = _pick_tile(n_tok, (512, 256, 128))
    g_blk = (3 * SB_WIDTH + 2 * GLA_KEY_WIDTH + GLA_WIDTH) // GLA_WIDTH
    ltri = jnp.asarray(np.tril(np.ones((tm, tm), np.float32), -1), BF16)
    const = lambda shape: pl.BlockSpec(shape, lambda i: (0, 0))
    tile = lambda w: pl.BlockSpec((tm, w), lambda i: (i, 0))
    return pl.pallas_call(
        _mix_kernel,
        grid=(n_tok // tm,),
        in_specs=[tile(SB_WIDTH), tile(GLA_WIDTH),
                  pl.BlockSpec((tm, GLA_WIDTH), lambda i: (i, g_blk)),
                  tile(D_MODEL), const((1, GLA_DV)), const((D_MODEL, D_MODEL)), const((1, D_MODEL)),
                  const((D_MODEL, 2 * LANES)), const((1, LANES)), const((tm, tm))],
        out_specs=[tile(D_MODEL), tile(D_MODEL), tile(LANES), tile(LANES), tile(LANES),
                   const((1, LANES))],
        out_shape=[jax.ShapeDtypeStruct((n_tok, D_MODEL), F32),
                   jax.ShapeDtypeStruct((n_tok, D_MODEL), F32),
                   jax.ShapeDtypeStruct((n_tok, LANES), jnp.int32),
                   jax.ShapeDtypeStruct((n_tok, LANES), F32),
                   jax.ShapeDtypeStruct((n_tok, LANES), jnp.int32),
                   jax.ShapeDtypeStruct((1, LANES), F32)],
        scratch_shapes=[pltpu.VMEM((1, LANES), F32)],
        compiler_params=_cparams(("arbitrary",)),
    )(o_sb, o_g, p32, x_all, gnorm, w_out, nffn, w_r, b_r, ltri)


DMA_ISSUE_UNROLL = 8


def _dispatch_kernel(dest_ref, x_ref, xs_ref, sem, *, rows):
    def start(r, carry):
        for k in range(TOP_K):
            pltpu.make_async_copy(x_ref.at[pl.ds(r, 1), :],
                                  xs_ref.at[pl.ds(dest_ref[0, 0, r * TOP_K + k], 1), :],
                                  sem).start(priority=k % 2)
        return carry

    lax.fori_loop(0, rows, start, 0, unroll=DMA_ISSUE_UNROLL)
    for k in range(TOP_K):
        pltpu.make_async_copy(x_ref, xs_ref.at[pl.ds(0, rows), :], sem).wait()


def _dispatch(xp, dest, n_slots):
    n_tok, width = xp.shape
    rows = _pick_tile(n_tok, (256, 128))
    dest3 = dest.reshape(n_tok // rows, 1, rows * TOP_K)
    return pl.pallas_call(
        functools.partial(_dispatch_kernel, rows=rows),
        grid=(n_tok // rows,),
        in_specs=[pl.BlockSpec((1, 1, rows * TOP_K), lambda i: (i, 0, 0), memory_space=pltpu.SMEM),
                  pl.BlockSpec((rows, width), lambda i: (i, 0))],
        out_specs=pl.BlockSpec(memory_space=pl.ANY),
        out_shape=jax.ShapeDtypeStruct((n_slots, width), xp.dtype),
        scratch_shapes=[pltpu.SemaphoreType.DMA(())],
        compiler_params=_cparams(("arbitrary",)),
    )(dest3, xp)


def _expert_changed(be_ref, m):
    prev = be_ref[jnp.maximum(m - 1, 0)]
    return (m == 0) | (be_ref[m] != prev)


def _moe_up_kernel(be_ref, nu_ref, xs_ref, wg_ref, wu_ref, bg_ref, bu_ref, h_ref, w16_ref, *, tf):
    m = pl.program_id(1)

    @pl.when(_expert_changed(be_ref, m))
    def _():
        w16_ref[:, :tf] = wg_ref[...].astype(BF16)
        w16_ref[:, tf:] = wu_ref[...].astype(BF16)

    @pl.when(m < nu_ref[0])
    def _():
        x = xs_ref[...].astype(BF16)
        gu = jnp.dot(x, w16_ref[...], preferred_element_type=F32)
        g = jnp.minimum(gu[:, :tf] + bg_ref[...], SWIGLU_LIMIT)
        u = jnp.clip(gu[:, tf:] + bu_ref[...], -SWIGLU_LIMIT, SWIGLU_LIMIT)
        h_ref[...] = ((u + 1.0) * (g * jax.nn.sigmoid(g * SWIGLU_ALPHA))).astype(h_ref.dtype)


def _moe_down_kernel(be_ref, nu_ref, h_ref, wd_ref, bd_ref, y_ref, w16_ref):
    m = pl.program_id(1)

    @pl.when(_expert_changed(be_ref, m))
    def _():
        w16_ref[...] = wd_ref[...].astype(BF16)

    @pl.when(m < nu_ref[0])
    def _():
        y_ref[...] = jnp.dot(h_ref[...], w16_ref[...], preferred_element_type=F32) + bd_ref[...]


def _moe_experts(xs, block_expert, n_used, w_gate_up, b_gate_up, w_down, b_down):
    n_slots = xs.shape[0]
    n_blocks = n_slots // MOE_TM
    tf = 1024
    n_f = D_FF // tf
    live = lambda m, nu: jnp.minimum(m, nu[0] - 1)
    h = pl.pallas_call(
        functools.partial(_moe_up_kernel, tf=tf),
        grid_spec=pltpu.PrefetchScalarGridSpec(
            num_scalar_prefetch=2,
            grid=(n_f, n_blocks),
            in_specs=[
                pl.BlockSpec((MOE_TM, D_MODEL), lambda j, m, be, nu: (live(m, nu), 0)),
                pl.BlockSpec((None, D_MODEL, tf), lambda j, m, be, nu: (be[m], 0, j)),
                pl.BlockSpec((None, D_MODEL, tf), lambda j, m, be, nu: (be[m], 0, n_f + j)),
                pl.BlockSpec((None, 1, tf), lambda j, m, be, nu: (be[m], 0, j)),
                pl.BlockSpec((None, 1, tf), lambda j, m, be, nu: (be[m], 0, n_f + j)),
            ],
            out_specs=pl.BlockSpec((MOE_TM, tf), lambda j, m, be, nu: (live(m, nu), j)),
            scratch_shapes=[pltpu.VMEM((D_MODEL, 2 * tf), BF16)],
        ),
        out_shape=jax.ShapeDtypeStruct((n_slots, D_FF), BF16),
        compiler_params=_cparams(("arbitrary", "arbitrary")),
    )(block_expert, n_used, xs, w_gate_up, w_gate_up, b_gate_up, b_gate_up)

    return pl.pallas_call(
        _moe_down_kernel,
        grid_spec=pltpu.PrefetchScalarGridSpec(
            num_scalar_prefetch=2,
            grid=(1, n_blocks),
            in_specs=[
                pl.BlockSpec((MOE_TM, D_FF), lambda j, m, be, nu: (live(m, nu), 0)),
                pl.BlockSpec((None, D_FF, D_MODEL), lambda j, m, be, nu: (be[m], 0, 0)),
                pl.BlockSpec((None, 1, D_MODEL), lambda j, m, be, nu: (be[m], 0, 0)),
            ],
            out_specs=pl.BlockSpec((MOE_TM, D_MODEL), lambda j, m, be, nu: (live(m, nu), 0)),
            scratch_shapes=[pltpu.VMEM((D_FF, D_MODEL), BF16)],
        ),
        out_shape=jax.ShapeDtypeStruct((n_slots, D_MODEL), F32),
        compiler_params=_cparams(("arbitrary", "arbitrary")),
    )(block_expert, n_used, h, w_down, b_down)


def _combine_kernel(dest_ref, y_ref, gate_ref, h_ref, nf_ref, outp_ref, outs_ref, buf_ref, sem, *,
                    rows, prompt_tiles):
    def start(r, carry):
        for k in range(TOP_K):
            pltpu.make_async_copy(y_ref.at[pl.ds(dest_ref[0, 0, r * TOP_K + k], 1), :],
                                  buf_ref.at[k, pl.ds(r, 1), :],
                                  sem).start(priority=k % 2)
        return carry

    lax.fori_loop(0, rows, start, 0, unroll=DMA_ISSUE_UNROLL)
    for k in range(TOP_K):
        pltpu.make_async_copy(y_ref.at[pl.ds(0, rows), :], buf_ref.at[k], sem).wait()

    gate = gate_ref[...]
    moe = buf_ref[0] * gate[:, 0:1]
    for k in range(1, TOP_K):
        moe = moe + buf_ref[k] * gate[:, k:k + 1]
    h = h_ref[...] + moe
    inv = lax.rsqrt(jnp.mean(h * h, axis=-1, keepdims=True) + RMS_EPS)
    y = (h * inv) * nf_ref[...]
    is_prompt = pl.program_id(0) < prompt_tiles

    @pl.when(is_prompt)
    def _():
        outp_ref[...] = y

    @pl.when(jnp.logical_not(is_prompt))
    def _():
        outs_ref[...] = y


def _combine(y_slots, dest, gate, h, norm_final, n_prompt):
    n_tok = h.shape[0]
    rows = _pick_tile(math.gcd(n_prompt, n_tok - n_prompt), (256, 128))
    prompt_tiles = n_prompt // rows
    dest3 = dest.reshape(n_tok // rows, 1, rows * TOP_K)
    return pl.pallas_call(
        functools.partial(_combine_kernel, rows=rows, prompt_tiles=prompt_tiles),
        grid=(n_tok // rows,),
        in_specs=[pl.BlockSpec((1, 1, rows * TOP_K), lambda i: (i, 0, 0), memory_space=pltpu.SMEM),
                  pl.BlockSpec(memory_space=pl.ANY),
                  pl.BlockSpec((rows, LANES), lambda i: (i, 0)),
                  pl.BlockSpec((rows, D_MODEL), lambda i: (i, 0)),
                  pl.BlockSpec((1, D_MODEL), lambda i: (0, 0))],
        out_specs=[pl.BlockSpec((rows, D_MODEL), lambda i: (jnp.minimum(i, prompt_tiles - 1), 0)),
                   pl.BlockSpec((rows, D_MODEL), lambda i: (jnp.maximum(i - prompt_tiles, 0), 0))],
        out_shape=[jax.ShapeDtypeStruct((n_prompt, D_MODEL), F32),
                   jax.ShapeDtypeStruct((n_tok - n_prompt, D_MODEL), F32)],
        scratch_shapes=[pltpu.VMEM((TOP_K, rows, D_MODEL), F32), pltpu.SemaphoreType.DMA(())],
        compiler_params=_cparams(("arbitrary",)),
    )(dest3, y_slots, gate, h, norm_final)


def kernel(x_prompt, x_sample, cache_sb_k, cache_sb_v, state_gla, meta_tokens, norm_mix, w_in, w_gk2, b_gk,
           gla_norm, w_out, norm_ffn, w_router, b_router, w_gate_up, b_gate_up, w_down, b_down, norm_final):
    assert x_prompt.shape[0] == 1 and norm_mix.shape[0] == 1, "one prompt stream, one layer"
    n_prompt = x_prompt.shape[1]
    n_batch, n_new = x_sample.shape[:2]
    n_past = cache_sb_k.shape[2]
    assert n_new == GLA_CHUNK and n_prompt % SB_TQ == 0 and n_past % (SB_TK * SB_CACHE_SPLIT) == 0
    n_tok = n_prompt + n_batch * n_new
    assert n_tok % META_BLOCK == 0

    x_all = jnp.concatenate([
        x_prompt[0], x_sample.reshape(n_batch * n_new, D_MODEL), meta_tokens.astype(F32),
        jnp.zeros((META_BLOCK - N_META, D_MODEL), F32)], axis=0)

    w_in0 = w_in[0]
    w_main = w_in0[:, :D_MAIN].astype(BF16)
    w_a = jnp.pad(w_in0[:, D_MAIN:], ((0, 0), (0, LANES - GLA_RANK))).astype(BF16)
    w_gk = jnp.pad(w_gk2[0], ((0, LANES - GLA_RANK), (0, 0)))
    colscale = np.ones((1, D_MAIN), np.float32)
    colscale[0, :SB_WIDTH] = SB_DIM ** -0.5 * LOG2E
    colscale[0, 3 * SB_WIDTH:3 * SB_WIDTH + GLA_KEY_WIDTH] = GLA_DK ** -0.5
    p32, p16, lg = _inproj(x_all, norm_mix, w_main, jnp.asarray(colscale), w_a, w_gk, b_gk)

    u = jnp.asarray(np.tril(np.ones((SB_TK, SB_TK), np.float32)), BF16)
    o_sb = _sb_prompt(p16, u, n_prompt, n_tok)
    o_sb = _sb_sample(p16, cache_sb_k[0].reshape(n_batch, n_past * SB_HEADS, SB_DIM),
                      cache_sb_v[0].reshape(n_batch, n_past * SB_HEADS, SB_DIM), u, o_sb, n_prompt, n_batch, n_new)

    tri, sel = _gla_constants()
    c = GLA_CHUNK
    zero_state = jnp.zeros((1, GLA_HEADS, GLA_DK, GLA_DV), F32)
    _, state_meta = _gla(p32, lg, zero_state, tri, sel, c, first_chunk=n_tok // c, n_seq_chunks=1,
                         chained=True, n_valid=N_META)
    o_g, state_p = _gla(p32, lg, state_meta, tri, sel, n_tok, first_chunk=0, n_seq_chunks=n_prompt // c,
                        chained=True)
    o_g, state_s = _gla(p32, lg, state_gla[0].astype(F32), tri, sel, o_g, first_chunk=n_prompt // c,
                        n_seq_chunks=n_batch, chained=False)

    w_r = jnp.pad(w_router[0].astype(F32), ((0, 0), (0, LANES - N_EXPERTS)))
    w_r_hi = w_r.astype(BF16)
    w_r = jnp.concatenate([w_r_hi, (w_r - w_r_hi.astype(F32)).astype(BF16)], axis=1)
    b_r = jnp.pad(b_router.astype(F32), ((0, 0), (0, LANES - N_EXPERTS)), constant_values=-jnp.inf)
    h, xp, top_idx, gate, rank, counts = _mix(o_sb, o_g, p32, x_all, gla_norm, w_out[0].astype(BF16),
                                              norm_ffn, w_r, b_r, n_tok)

    counts = counts[0, :N_EXPERTS].astype(jnp.int32)
    padded = (counts + MOE_TM - 1) // MOE_TM * MOE_TM
    ends = jnp.cumsum(padded)
    n_blocks = -(-n_tok * TOP_K // MOE_TM) + N_EXPERTS
    dest = ((ends - padded)[top_idx[:, :TOP_K]] + rank[:, :TOP_K]).astype(jnp.int32)
    block_start = jnp.arange(n_blocks, dtype=jnp.int32) * MOE_TM
    block_expert = jnp.minimum(jnp.sum((ends[None, :] <= block_start[:, None]).astype(jnp.int32), axis=1),
                               N_EXPERTS - 1)
    n_used = (ends[-1:] // MOE_TM).astype(jnp.int32)

    xs = _dispatch(xp, dest, n_blocks * MOE_TM)
    y_slots = _moe_experts(xs, block_expert, n_used, w_gate_up[0], b_gate_up[0][:, None, :],
                           w_down[0], b_down[0][:, None, :])
    y_prompt, y_sample = _combine(y_slots, dest, gate, h, norm_final[None, :], n_prompt)
    y_prompt = y_prompt[None]
    y_sample = y_sample.reshape(n_batch, n_new, D_MODEL)
    k_lo, v_lo = SB_WIDTH, 2 * SB_WIDTH

    def kv_prompt(lo):
        rows = jnp.concatenate([p32[n_tok:n_tok + N_META, lo:lo + SB_WIDTH], p32[:n_prompt, lo:lo + SB_WIDTH]])
        return rows.reshape(1, 1, N_META + n_prompt, SB_HEADS, SB_DIM)

    def kv_sample(lo):
        return p32[n_prompt:n_tok, lo:lo + SB_WIDTH].reshape(1, n_batch, n_new, SB_HEADS, SB_DIM)

    return (y_prompt, y_sample, kv_prompt(k_lo), kv_prompt(v_lo), state_p[None],
            kv_sample(k_lo), kv_sample(v_lo), state_s[None])
```
